```python
import math
import jax, jax.numpy as jnp
from jax import lax
import numpy as np

D_MODEL = 2048
BATCH = 1
SEQ = 8192
DEPTH = 2
DEC_BATCH = 32
DEC_SEQ = 1
PAST_LEN = 8192
PAGE_SIZE = 128

F32 = jnp.float32
HEAD_DIM = 128
A_HEADS = D_MODEL // (4 * HEAD_DIM)
A_HALF = HEAD_DIM // 2
A_WIDTH = A_HEADS * HEAD_DIM
B_HEADS = D_MODEL // (2 * HEAD_DIM)
B_KV = 2
B_GROUP = B_HEADS // B_KV
B_WIDTH = B_HEADS * HEAD_DIM
KV_WIDTH = B_KV * HEAD_DIM
CMP_STRIDE = 16
CMP_LEN = 2 * CMP_STRIDE
SEL_LEN = 64
SEL_RATIO = SEL_LEN // CMP_STRIDE
SEL_TOP = 16
WINDOW = 512
C_WIDTH = D_MODEL // 4
C_WINDOWS = (2, 4, 8, 16)
C_GROUPS = len(C_WINDOWS)
C_GW = C_WIDTH // C_GROUPS
POOL_HIST = max(C_WINDOWS) - 1
MIX_WIDTH = A_WIDTH + B_WIDTH + C_WIDTH
CACHE_HEADS = A_HEADS + 2 * B_KV
PROJ_SIZES = (A_WIDTH, A_WIDTH, A_WIDTH, B_WIDTH, KV_WIDTH, KV_WIDTH, KV_WIDTH, KV_WIDTH, KV_WIDTH, KV_WIDTH, 3 * B_HEADS, C_WIDTH)
PROJ_WIDTH = sum(PROJ_SIZES)
D_FF = ((8 * D_MODEL // 3 + 255) // 256) * 256
N_EXPERTS = 8
TOP_K = 2
N_DENSE = (DEPTH + 1) // 2
N_MOE = DEPTH // 2
ROPE_THETA = 10000.0
LN_EPS = 1e-5
ALPHA = (2 * DEPTH) ** 0.25
BETA = (8 * DEPTH) ** -0.25
Q_BLOCK = 128
NEG = -1e30
BIG = 1e30

kernel_name = "hymba_diff_nsa_pool_decoder_step"


def _normal(k, shape, scale=1.0):
    return jax.random.normal(k, shape, F32) * scale


def rope(x, pos):
    half = x.shape[-1] // 2
    inv = ROPE_THETA ** (-jnp.arange(half, dtype=F32) / half)
    ang = pos.astype(F32)[:, None] * inv[None, :]
    shape = (pos.shape[0],) + (1,) * (x.ndim - 3) + (half,)
    cos = jnp.cos(ang).reshape(shape)
    sin = jnp.sin(ang).reshape(shape)
    xf = x.astype(F32)
    x1, x2 = xf[..., :half], xf[..., half:]
    return jnp.concatenate([x1 * cos - x2 * sin, x1 * sin + x2 * cos], -1).astype(x.dtype)


def layer_norm(x, g, b):
    xf = x.astype(F32)
    mu = jnp.mean(xf, -1, keepdims=True)
    xc = xf - mu
    var = jnp.mean(xc * xc, -1, keepdims=True)
    return (xc * lax.rsqrt(var + LN_EPS) * g.astype(F32) + b.astype(F32)).astype(x.dtype)


def query_block(n):
    return Q_BLOCK if n % Q_BLOCK == 0 else n


def to_blocks(a, blk):
    b, s = a.shape[:2]
    return jnp.swapaxes(a.reshape((b, s // blk, blk) + a.shape[2:]), 0, 1)


def from_blocks(a):
    a = jnp.swapaxes(a, 0, 1)
    return a.reshape((a.shape[0], a.shape[1] * a.shape[2]) + a.shape[3:])


def project(x, w_in):
    b, s = x.shape[:2]
    p = jnp.einsum('bsd,dc->bsc', x, w_in)
    parts, o = [], 0
    for n in PROJ_SIZES:
        parts.append(p[..., o:o + n])
        o += n
    qa, ka, va, qb, kc, vc, ks, vs, kw, vw, gb, u = parts
    kvs = lambda t: t.reshape(b, s, B_KV, HEAD_DIM)
    return (qa.reshape(b, s, A_HEADS, 2, A_HALF), ka.reshape(b, s, A_HEADS, 2, A_HALF),
            va.reshape(b, s, A_HEADS, HEAD_DIM), qb.reshape(b, s, B_HEADS, HEAD_DIM),
            kvs(kc), kvs(vc), kvs(ks), kvs(vs), kvs(kw), kvs(vw),
            jax.nn.sigmoid(gb.astype(F32)).reshape(b, s, B_HEADS, 3), u)


def diff_attention(q, k, v, q_pos, k_pos, lam_vec, lam_init, norm_g):
    blk = query_block(q.shape[1])
    scale = A_HALF ** -0.5
    lv = lam_vec.astype(F32)
    lam = jnp.exp(jnp.sum(lv[0] * lv[1])) - jnp.exp(jnp.sum(lv[2] * lv[3])) + lam_init

    def one(args):
        qi, pi = args
        s = jnp.einsum('bqhcd,bkhcd->bhcqk', qi, k).astype(F32) * scale
        mask = k_pos[None, :] <= pi[:, None]
        p = jax.nn.softmax(jnp.where(mask, s, NEG), axis=-1)
        p = p[:, :, 0] - lam * p[:, :, 1]
        return jnp.einsum('bhqk,bkhd->bqhd', p.astype(v.dtype), v)

    o = from_blocks(lax.map(one, (to_blocks(q, blk), q_pos.reshape(-1, blk)))).astype(F32)
    o = o * lax.rsqrt(jnp.mean(o * o, -1, keepdims=True) + LN_EPS) * norm_g.astype(F32) * (1.0 - lam_init)
    return o.astype(v.dtype)


def compress(rows):
    b, t = rows.shape[:2]
    nch = -(-t // CMP_STRIDE)
    r = jnp.pad(rows.astype(F32), ((0, 0), (0, nch * CMP_STRIDE - t), (0, 0), (0, 0)))
    chunk = r.reshape((b, nch, CMP_STRIDE) + rows.shape[2:]).sum(2)
    blocks = (chunk[:, :-1] + chunk[:, 1:]) / CMP_LEN
    ends = jnp.arange(nch - 1, dtype=jnp.int32) * CMP_STRIDE + (CMP_LEN - 1)
    return blocks.astype(rows.dtype), ends


def select_blocks(rows):
    b, t = rows.shape[:2]
    nsel = -(-t // SEL_LEN)
    r = jnp.pad(rows, ((0, 0), (0, nsel * SEL_LEN - t), (0, 0), (0, 0)))
    return r.reshape(b, nsel, SEL_LEN, B_KV, HEAD_DIM).transpose(0, 3, 1, 2, 4)


def window_blocks(rows, blk):
    nb = rows.shape[1] // blk
    rp = jnp.pad(rows, ((0, 0), (WINDOW, 0), (0, 0), (0, 0)))
    idx = jnp.arange(nb)[:, None] * blk + jnp.arange(blk + WINDOW)[None, :]
    return jnp.moveaxis(rp[:, idx], 1, 0), idx - WINDOW


def nsa_attention(q, q_pos, gates, kcb, vcb, cend, ksb, vsb, kw_blk, vw_blk, wpos_blk):
    b = q.shape[0]
    blk = query_block(q.shape[1])
    scale = HEAD_DIM ** -0.5
    nsel = ksb.shape[2]
    ntop = min(SEL_TOP, nsel)
    back = SEL_RATIO * (nsel + 1) - 1 - kcb.shape[1]
    bi = jnp.arange(b)[:, None, None, None]
    ki = jnp.arange(B_KV)[None, :, None, None]
    kidx = jnp.arange(nsel)[None, :]
    lpos = jnp.arange(SEL_LEN)

    def one(args):
        qi, pi, gi, kwi, vwi, wp = args
        qg = qi.reshape(b, blk, B_KV, B_GROUP, HEAD_DIM)
        sc = jnp.einsum('bqkgd,bjkd->bkgqj', qg, kcb).astype(F32) * scale
        mc = cend[None, :] <= pi[:, None]
        pc = jax.nn.softmax(jnp.where(mc, sc, NEG), -1) * jnp.any(mc, -1)[:, None]
        oc = jnp.einsum('bkgqj,bjkd->bqkgd', pc.astype(vcb.dtype), vcb)
        imp = jnp.pad(pc.sum(2), ((0, 0), (0, 0), (0, 0), (1, back)))
        imp = imp.reshape(b, B_KV, blk, nsel + 1, SEL_RATIO)
        imp = imp[..., :nsel, :].sum(-1) + imp[..., 1:, 0]
        cur = (pi // SEL_LEN)[:, None]
        imp = jnp.where(kidx > cur, NEG, imp)
        forced = (kidx == 0) | (kidx == cur) | (kidx == cur - 1)
        imp = jnp.where(forced, BIG, imp)
        _, idx = lax.top_k(imp, ntop)
        kg = ksb[bi, ki, idx]
        vg = vsb[bi, ki, idx]
        ss = jnp.einsum('bqkgd,bkqnld->bkgqnl', qg, kg).astype(F32) * scale
        ms = (idx[..., None] * SEL_LEN + lpos) <= pi[None, None, :, None, None]
        ss = jnp.where(ms[:, :, None], ss, NEG)
        ps = jax.nn.softmax(ss.reshape(ss.shape[:4] + (-1,)), -1).reshape(ss.shape)
        osel = jnp.einsum('bkgqnl,bkqnld->bqkgd', ps.astype(vg.dtype), vg)
        sw = jnp.einsum('bqkgd,bwkd->bkgqw', qg, kwi).astype(F32) * scale
        mw = (wp[None, :] <= pi[:, None]) & (wp[None, :] > pi[:, None] - WINDOW) & (wp[None, :] >= 0)
        pw = jax.nn.softmax(jnp.where(mw, sw, NEG), -1)
        ow = jnp.einsum('bkgqw,bwkd->bqkgd', pw.astype(vwi.dtype), vwi)
        g = gi.reshape(b, blk, B_KV, B_GROUP, 3).astype(oc.dtype)
        o = g[..., 0:1] * oc + g[..., 1:2] * osel + g[..., 2:3] * ow
        return o.reshape(b, blk, B_HEADS, HEAD_DIM)

    out = lax.map(one, (to_blocks(q, blk), q_pos.reshape(-1, blk), to_blocks(gates, blk), kw_blk, vw_blk, wpos_blk))
    return from_blocks(out)


def pool_mixer(u_ext, n_prev, pool_w, pool_scale):
    b, length = u_ext.shape[:2]
    uf = u_ext.astype(F32)
    cs = jnp.pad(jnp.cumsum(uf, axis=1), ((0, 0), (1, 0), (0, 0)))
    i = jnp.arange(n_prev, length)
    outs = []
    for g, w in enumerate(C_WINDOWS):
        lo = jnp.maximum(i + 1 - w, 0)
        sl = slice(g * C_GW, (g + 1) * C_GW)
        mean = (cs[:, i + 1, sl] - cs[:, lo, sl]) / (i + 1 - lo).astype(F32)[None, :, None]
        outs.append(mean - uf[:, n_prev:, sl])
    d = jnp.stack(outs, axis=2).astype(u_ext.dtype)
    o = jnp.einsum('bsgc,gce->bsge', d, pool_w).reshape(b, length - n_prev, C_WIDTH)
    return o * pool_scale


def token_mixers(qa, qb, gb, q_pos, ka_all, va_all, kc_all, vc_all, ks_all, vs_all, k_pos,
                 kw_blk, vw_blk, wpos_blk, u_ext, n_prev, w_out, lam_vec, lam_init, norm_g, pool_w, pool_scale):
    b, s = qa.shape[:2]
    oa = diff_attention(qa, ka_all, va_all, q_pos, k_pos, lam_vec, lam_init, norm_g)
    kcb, cend = compress(kc_all)
    kcb = rope(kcb, cend)
    vcb, _ = compress(vc_all)
    ob = nsa_attention(qb, q_pos, gb, kcb, vcb, cend, select_blocks(ks_all), select_blocks(vs_all), kw_blk, vw_blk, wpos_blk)
    oc = pool_mixer(u_ext, n_prev, pool_w, pool_scale)
    o = jnp.concatenate([oa.reshape(b, s, A_WIDTH), ob.reshape(b, s, B_WIDTH), oc], -1)
    return jnp.einsum('bsc,cd->bsd', o, w_out)


def prompt_mixer(x, w_in, w_out, lam_vec, lam_init, norm_g, pool_w, pool_scale):
    b, s = x.shape[:2]
    pos = jnp.arange(s, dtype=jnp.int32)
    qa, ka, va, qb, kc, vc, ks, vs, kw, vw, gb, u = project(x, w_in)
    qa, ka, qb, ks, kw = rope(qa, pos), rope(ka, pos), rope(qb, pos), rope(ks, pos), rope(kw, pos)
    blk = query_block(s)
    kwb, wpos = window_blocks(kw, blk)
    vwb, _ = window_blocks(vw, blk)
    y = token_mixers(qa, qb, gb, pos, ka, va, kc, vc, ks, vs, pos, kwb, vwb, wpos, u, 0,
                     w_out, lam_vec, lam_init, norm_g, pool_w, pool_scale)
    wb = min(WINDOW, s)
    k_rows = jnp.concatenate([ka.reshape(b, s, A_HEADS, HEAD_DIM), kc, ks], 2)
    v_rows = jnp.concatenate([va, vc, vs], 2)
    win = jnp.stack([kw[:, s - wb:], vw[:, s - wb:]], 2)
    return y, k_rows, v_rows, win, u[:, s - POOL_HIST:]


def sample_mixer(x, ck, cv, win_l, pool_l, page_table, w_in, w_out, lam_vec, lam_init, norm_g, pool_w, pool_scale):
    b, s = x.shape[:2]
    past = page_table.shape[1] * ck.shape[1]
    pos = past + jnp.arange(s, dtype=jnp.int32)
    qa, ka, va, qb, kc, vc, ks, vs, kw, vw, gb, u = project(x, w_in)
    qa, ka, qb, ks, kw = rope(qa, pos), rope(ka, pos), rope(qb, pos), rope(ks, pos), rope(kw, pos)
    kp = ck[page_table].reshape(b, past, CACHE_HEADS, HEAD_DIM)
    vp = cv[page_table].reshape(b, past, CACHE_HEADS, HEAD_DIM)
    a0, a1 = A_HEADS, A_HEADS + B_KV
    ka_all = jnp.concatenate([kp[:, :, :a0].reshape(b, past, A_HEADS, 2, A_HALF), ka], 1)
    va_all = jnp.concatenate([vp[:, :, :a0], va], 1)
    kc_all = jnp.concatenate([kp[:, :, a0:a1], kc], 1)
    vc_all = jnp.concatenate([vp[:, :, a0:a1], vc], 1)
    ks_all = jnp.concatenate([kp[:, :, a1:], ks], 1)
    vs_all = jnp.concatenate([vp[:, :, a1:], vs], 1)
    k_pos = jnp.arange(past + s, dtype=jnp.int32)
    wb = win_l.shape[1]
    kw_all = jnp.concatenate([win_l[:, :, 0], kw], 1)
    vw_all = jnp.concatenate([win_l[:, :, 1], vw], 1)
    wpos = past - wb + jnp.arange(wb + s, dtype=jnp.int32)
    nb = s // query_block(s)
    kwb = jnp.broadcast_to(kw_all[None], (nb,) + kw_all.shape)
    vwb = jnp.broadcast_to(vw_all[None], (nb,) + vw_all.shape)
    wposb = jnp.broadcast_to(wpos[None], (nb, wb + s))
    u_ext = jnp.concatenate([pool_l, u], 1)
    y = token_mixers(qa, qb, gb, pos, ka_all, va_all, kc_all, vc_all, ks_all, vs_all, k_pos,
                     kwb, vwb, wposb, u_ext, POOL_HIST, w_out, lam_vec, lam_init, norm_g, pool_w, pool_scale)
    k_rows = jnp.concatenate([ka.reshape(b, s, A_HEADS, HEAD_DIM), kc, ks], 2)
    v_rows = jnp.concatenate([va, vc, vs], 2)
    win = jnp.stack([kw_all[:, s:], vw_all[:, s:]], 2)
    return y, k_rows, v_rows, win, u_ext[:, s:]


def swiglu(x, wg, wu, wd):
    h = jax.nn.silu(jnp.einsum('bsd,df->bsf', x, wg)) * jnp.einsum('bsd,df->bsf', x, wu)
    return jnp.einsum('bsf,fd->bsd', h, wd)


def moe_ffn(x, router, wg, wu, wd):
    logits = jnp.einsum('bsd,de->bse', x, router).astype(F32)
    top_v, top_i = lax.top_k(logits, TOP_K)
    gate = jnp.sum(jax.nn.one_hot(top_i, N_EXPERTS, dtype=F32) * jax.nn.softmax(top_v, -1)[..., None], axis=-2).astype(x.dtype)
    y = jnp.zeros_like(x)
    for e in range(N_EXPERTS):
        y = y + gate[..., e:e + 1] * swiglu(x, wg[e], wu[e], wd[e])
    return y


def channel_mixer(h, l, ffn_w_gate, ffn_w_up, ffn_w_down, moe_router, moe_w_gate, moe_w_up, moe_w_down):
    i = l // 2
    if l % 2 == 0:
        return swiglu(h, ffn_w_gate[i], ffn_w_up[i], ffn_w_down[i])
    return moe_ffn(h, moe_router[i], moe_w_gate[i], moe_w_up[i], moe_w_down[i])


def setup_inputs(seed: int = 0) -> dict:
    key = jax.random.key(seed)
    ks = jax.random.split(key, 24)
    n_pages = PAST_LEN // PAGE_SIZE
    n_pool = (5 * DEC_BATCH * n_pages + 3) // 4
    wbuf = min(WINDOW, PAST_LEN)
    page_table = jax.random.permutation(ks[6], n_pool)[:DEC_BATCH * n_pages].reshape(DEC_BATCH, n_pages).astype(jnp.int32)
    return {
        'x_prompt': _normal(ks[0], (BATCH, SEQ, D_MODEL)),
        'x_sample': _normal(ks[1], (DEC_BATCH, DEC_SEQ, D_MODEL)),
        'cache_k': _normal(ks[2], (DEPTH, n_pool, PAGE_SIZE, CACHE_HEADS, HEAD_DIM)),
        'cache_v': _normal(ks[3], (DEPTH, n_pool, PAGE_SIZE, CACHE_HEADS, HEAD_DIM)),
        'state_win': _normal(ks[4], (DEPTH, DEC_BATCH, wbuf, 2, B_KV, HEAD_DIM)),
        'state_pool': _normal(ks[5], (DEPTH, DEC_BATCH, POOL_HIST, C_WIDTH)),
        'page_table': page_table,
        'w_in': _normal(ks[7], (DEPTH, D_MODEL, PROJ_WIDTH), D_MODEL ** -0.5),
        'w_out': _normal(ks[8], (DEPTH, MIX_WIDTH, D_MODEL), BETA * MIX_WIDTH ** -0.5),
        'diff_lambda': _normal(ks[9], (DEPTH, 4, A_HALF), 0.1),
        'diff_norm_g': 1.0 + _normal(ks[10], (DEPTH, HEAD_DIM), 0.02),
        'pool_w': _normal(ks[11], (DEPTH, C_GROUPS, C_GW, C_GW), C_GW ** -0.5),
        'pool_scale': 1.0 + _normal(ks[12], (DEPTH, C_WIDTH), 0.02),
        'ln1_g': 1.0 + _normal(ks[13], (DEPTH, D_MODEL), 0.02),
        'ln1_b': _normal(ks[14], (DEPTH, D_MODEL), 0.02),
        'ln2_g': 1.0 + _normal(ks[15], (DEPTH, D_MODEL), 0.02),
        'ln2_b': _normal(ks[16], (DEPTH, D_MODEL), 0.02),
        'ffn_w_gate': _normal(ks[17], (N_DENSE, D_MODEL, D_FF), D_MODEL ** -0.5),
        'ffn_w_up': _normal(ks[18], (N_DENSE, D_MODEL, D_FF), D_MODEL ** -0.5),
        'ffn_w_down': _normal(ks[19], (N_DENSE, D_FF, D_MODEL), BETA * D_FF ** -0.5),
        'moe_router': _normal(ks[20], (N_MOE, D_MODEL, N_EXPERTS), D_MODEL ** -0.5),
        'moe_w_gate': _normal(ks[21], (N_MOE, N_EXPERTS, D_MODEL, D_FF), D_MODEL ** -0.5),
        'moe_w_up': _normal(ks[22], (N_MOE, N_EXPERTS, D_MODEL, D_FF), D_MODEL ** -0.5),
        'moe_w_down': _normal(ks[23], (N_MOE, N_EXPERTS, D_FF, D_MODEL), BETA * D_FF ** -0.5),
    }


def reference(x_prompt, x_sample, cache_k, cache_v, state_win, state_pool, page_table,
              w_in, w_out, diff_lambda, diff_norm_g, pool_w, pool_scale,
              ln1_g, ln1_b, ln2_g, ln2_b, ffn_w_gate, ffn_w_up, ffn_w_down,
              moe_router, moe_w_gate, moe_w_up, moe_w_down):
    xp, xs = x_prompt, x_sample
    pk, pv, pw, pp, sk, sv, sw, sp = [], [], [], [], [], [], [], []
    for l in range(DEPTH):
        lam_init = 0.8 - 0.6 * math.exp(-0.3 * l)
        hp, k_r, v_r, w_r, p_r = prompt_mixer(xp, w_in[l], w_out[l], diff_lambda[l], lam_init, diff_norm_g[l], pool_w[l], pool_scale[l])
        pk.append(k_r); pv.append(v_r); pw.append(w_r); pp.append(p_r)
        hs, k_r, v_r, w_r, p_r = sample_mixer(xs, cache_k[l], cache_v[l], state_win[l], state_pool[l], page_table,
                                              w_in[l], w_out[l], diff_lambda[l], lam_init, diff_norm_g[l], pool_w[l], pool_scale[l])
        sk.append(k_r); sv.append(v_r); sw.append(w_r); sp.append(p_r)
        xp = layer_norm(ALPHA * xp + hp, ln1_g[l], ln1_b[l])
        xs = layer_norm(ALPHA * xs + hs, ln1_g[l], ln1_b[l])
        fp = channel_mixer(xp, l, ffn_w_gate, ffn_w_up, ffn_w_down, moe_router, moe_w_gate, moe_w_up, moe_w_down)
        fs = channel_mixer(xs, l, ffn_w_gate, ffn_w_up, ffn_w_down, moe_router, moe_w_gate, moe_w_up, moe_w_down)
        xp = layer_norm(ALPHA * xp + fp, ln2_g[l], ln2_b[l])
        xs = layer_norm(ALPHA * xs + fs, ln2_g[l], ln2_b[l])
    return (xp, xs, jnp.stack(pk), jnp.stack(pv), jnp.stack(pw), jnp.stack(pp),
            jnp.stack(sk), jnp.stack(sv), jnp.stack(sw), jnp.stack(sp))
```

```python
import functools
import math

import jax
import jax.numpy as jnp
from jax import lax
from jax.experimental import pallas as pl
from jax.experimental.pallas import tpu as pltpu

F32 = jnp.float32
BF16 = jnp.bfloat16
I32 = jnp.int32

HEAD_DIM = 128
A_HEADS = 4
A_HALF = 64
A_WIDTH = 512
B_HEADS = 8
B_KV = 2
B_GROUP = 4
B_WIDTH = 1024
KV_WIDTH = 256
CMP_STRIDE = 16
CMP_LEN = 32
SEL_LEN = 64
SEL_TOP = 16
WINDOW = 512
C_WIDTH = 512
C_WINDOWS = (2, 4, 8, 16)
C_GW = 128
POOL_HIST = 15
CACHE_HEADS = 8
PAGE_SIZE = 128
N_EXPERTS = 8
TOP_K = 2
ROPE_THETA = 10000.0
LN_EPS = 1e-5
NEG = -1e30
BIG = 1e30

LANES = 128
VMEM_LIMIT = 56 * 2 ** 20


def _cp(*sem):
    return pltpu.CompilerParams(dimension_semantics=sem, vmem_limit_bytes=VMEM_LIMIT)


def _dot(a, b):
    return jnp.dot(a, b, preferred_element_type=F32)


def _dot_nt(a, b):
    return lax.dot_general(a, b, (((1,), (1,)), ((), ())), preferred_element_type=F32)


def _split3(x):
    hi = x.astype(BF16)
    r1 = x - hi.astype(F32)
    mid = r1.astype(BF16)
    lo = (r1 - mid.astype(F32)).astype(BF16)
    return hi, mid, lo


def _rope_tables(pos):
    lane = jnp.arange(LANES)
    p = pos.astype(F32)[:, None]
    inv128 = ROPE_THETA ** (-(lane % 64).astype(F32) / 64)
    a128 = p * inv128[None, :]
    cos128 = jnp.cos(a128)
    sin128 = jnp.where(lane[None, :] < 64, -jnp.sin(a128), jnp.sin(a128))
    inv64 = ROPE_THETA ** (-(lane % 32).astype(F32) / 32)
    a64 = p * inv64[None, :]
    cos64 = jnp.cos(a64)
    sin64 = jnp.where((lane[None, :] % 64) < 32, -jnp.sin(a64), jnp.sin(a64))
    return cos128, sin128, cos64, sin64


def _proj_kernel(x_ref, w_ref, c128_ref, s128_ref, c64_ref, s64_ref, *out_refs, kinds, scales, want_f32, want_bf16):
    acc = _dot(x_ref[...], w_ref[...])
    tm = acc.shape[0]
    lane = lax.broadcasted_iota(I32, (tm, LANES), 1)
    first = (lane % 64) < 32
    refs = list(out_refs)
    f_ref = refs.pop(0) if want_f32 else None
    b_ref = refs.pop(0) if want_bf16 else None
    for c, kind in enumerate(kinds):
        blk = acc[:, c * LANES:(c + 1) * LANES]
        if kind == "r128":
            blk = blk * c128_ref[...] + pltpu.roll(blk, 64, 1) * s128_ref[...]
        elif kind == "r64":
            partner = jnp.where(first, pltpu.roll(blk, 96, 1), pltpu.roll(blk, 32, 1))
            blk = blk * c64_ref[...] + partner * s64_ref[...]
        elif kind == "sig":
            blk = 1.0 / (1.0 + jnp.exp(-blk))
        if f_ref is not None:
            f_ref[:, c * LANES:(c + 1) * LANES] = blk
        if b_ref is not None:
            sb = blk if scales[c] == 1.0 else blk * scales[c]
            b_ref[:, c * LANES:(c + 1) * LANES] = sb.astype(BF16)


def _proj(x_bf, w_bf, tabs, kinds, scales=None, want_f32=True, want_bf16=True, tm=512):
    m, d = x_bf.shape
    n = w_bf.shape[1]
    tm = min(tm, m)
    scales = tuple(scales) if scales is not None else (1.0,) * (n // LANES)
    outs, specs = [], []
    if want_f32:
        outs.append(jax.ShapeDtypeStruct((m, n), F32))
        specs.append(pl.BlockSpec((tm, n), lambda i: (i, 0)))
    if want_bf16:
        outs.append(jax.ShapeDtypeStruct((m, n), BF16))
        specs.append(pl.BlockSpec((tm, n), lambda i: (i, 0)))
    tab = pl.BlockSpec((tm, LANES), lambda i: (i, 0))
    res = pl.pallas_call(
        functools.partial(_proj_kernel, kinds=tuple(kinds), scales=scales, want_f32=want_f32, want_bf16=want_bf16),
        grid=(m // tm,),
        in_specs=[pl.BlockSpec((tm, d), lambda i: (i, 0)), pl.BlockSpec((d, n), lambda i: (0, 0)), tab, tab, tab, tab],
        out_specs=specs, out_shape=outs, compiler_params=_cp("parallel"), name="proj",
    )(x_bf, w_bf, *tabs)
    return res


def _diff_finalize(a1, l1, a2, l2, lam_ref, g_ref, lam_init):
    lv = lam_ref[...]
    la = jnp.sum(lv[0:1] * lv[1:2], axis=1, keepdims=True)
    lb = jnp.sum(lv[2:3] * lv[3:4], axis=1, keepdims=True)
    lam = jnp.exp(la) - jnp.exp(lb) + lam_init
    o = a1 / l1 - lam * (a2 / l2)
    o = o * lax.rsqrt(jnp.mean(o * o, axis=-1, keepdims=True) + LN_EPS) * g_ref[...] * (1.0 - lam_init)
    return o


def _diff_attn_kernel(q_ref, k_ref, v_ref, lam_ref, g_ref, o_ref, *, t, lam_init):
    qi = pl.program_id(1)
    q = q_ref[...]
    lane = lax.broadcasted_iota(I32, q.shape, 1)
    zero = jnp.zeros_like(q)
    q1 = jnp.where(lane < A_HALF, q, zero)
    q2 = jnp.where(lane < A_HALF, zero, q)
    rowi = lax.broadcasted_iota(I32, (t, t), 0)
    coli = lax.broadcasted_iota(I32, (t, t), 1)

    def update(s, m, l, a, v):
        mn = jnp.maximum(m, jnp.max(s, axis=-1, keepdims=True))
        al = jnp.exp(m - mn)
        p = jnp.exp(s - mn)
        l = al * l + jnp.sum(p, axis=-1, keepdims=True)
        a = al * a + _dot(p.astype(BF16), v)
        return mn, l, a

    def tile(ti, carry, masked):
        m1, l1, a1, m2, l2, a2 = carry
        off = pl.multiple_of(ti * t, t)
        k = k_ref[pl.ds(off, t), :]
        v = v_ref[pl.ds(off, t), :]
        s1 = _dot_nt(q1, k)
        s2 = _dot_nt(q2, k)
        if masked:
            s1 = jnp.where(coli <= rowi, s1, NEG)
            s2 = jnp.where(coli <= rowi, s2, NEG)
        m1, l1, a1 = update(s1, m1, l1, a1, v)
        m2, l2, a2 = update(s2, m2, l2, a2, v)
        return m1, l1, a1, m2, l2, a2

    mi = jnp.full((t, 1), NEG, F32)
    li = jnp.zeros((t, 1), F32)
    ai = jnp.zeros((t, HEAD_DIM), F32)
    carry = lax.fori_loop(0, qi, lambda ti, c: tile(ti, c, False), (mi, li, ai, mi, li, ai))
    m1, l1, a1, m2, l2, a2 = tile(qi, carry, True)
    o_ref[...] = _diff_finalize(a1, l1, a2, l2, lam_ref, g_ref, lam_init).astype(o_ref.dtype)


def _diff_attn_prompt(q_bf, k_bf, v_bf, lam_vec, norm_g, lam_init, t=512):
    s = q_bf.shape[0]
    t = min(t, s)
    return pl.pallas_call(
        functools.partial(_diff_attn_kernel, t=t, lam_init=lam_init),
        grid=(A_HEADS, s // t),
        in_specs=[pl.BlockSpec((t, HEAD_DIM), lambda h, i: (i, h)),
                  pl.BlockSpec((s, HEAD_DIM), lambda h, i: (0, h)),
                  pl.BlockSpec((s, HEAD_DIM), lambda h, i: (0, h)),
                  pl.BlockSpec((4, A_HALF), lambda h, i: (0, 0)),
                  pl.BlockSpec((1, HEAD_DIM), lambda h, i: (0, 0))],
        out_specs=pl.BlockSpec((t, HEAD_DIM), lambda h, i: (i, h)),
        out_shape=jax.ShapeDtypeStruct((s, A_WIDTH), BF16),
        compiler_params=_cp("parallel", "parallel"), name="diff_attn",
    )(q_bf, k_bf, v_bf, lam_vec, norm_g.reshape(1, HEAD_DIM))


def _compress_kernel(kc_ref, vc_ref, cos_ref, sin_ref, kcb_ref, vcb_ref, kch, vch, *, nc, steps):
    i = pl.program_id(0)
    rows = kc_ref.shape[0]
    cpb = rows // CMP_STRIDE
    off = pl.multiple_of(i * cpb, cpb)
    kch[pl.ds(off, cpb), :] = jnp.sum(kc_ref[...].reshape(cpb, CMP_STRIDE, KV_WIDTH), axis=1)
    vch[pl.ds(off, cpb), :] = jnp.sum(vc_ref[...].reshape(cpb, CMP_STRIDE, KV_WIDTH), axis=1)

    @pl.when(i == steps - 1)
    def _():
        last = lax.broadcasted_iota(I32, (nc, KV_WIDTH), 0) == nc - 1

        def blocks(ch_ref):
            ch = ch_ref[...]
            nxt = pltpu.roll(ch, nc - 1, 0)
            return jnp.where(last, 0.0, (ch + nxt) / CMP_LEN)

        kb = blocks(kch)
        for c in range(B_KV):
            blk = kb[:, c * LANES:(c + 1) * LANES]
            blk = blk * cos_ref[...] + pltpu.roll(blk, 64, 1) * sin_ref[...]
            kcb_ref[:, c * LANES:(c + 1) * LANES] = blk.astype(BF16)
        vcb_ref[...] = blocks(vch).astype(BF16)


def _compress_prompt(k_f32, v_f32, cos_c, sin_c):
    s = k_f32.shape[0]
    nc = s // CMP_STRIDE
    steps = 4 if s >= 2048 else 1
    rows = s // steps
    return pl.pallas_call(
        functools.partial(_compress_kernel, nc=nc, steps=steps),
        grid=(steps,),
        in_specs=[pl.BlockSpec((rows, KV_WIDTH), lambda i: (i, 2)),
                  pl.BlockSpec((rows, KV_WIDTH), lambda i: (i, 2)),
                  pl.BlockSpec((nc, LANES), lambda i: (0, 0)),
                  pl.BlockSpec((nc, LANES), lambda i: (0, 0))],
        out_specs=[pl.BlockSpec((nc, KV_WIDTH), lambda i: (0, 0)), pl.BlockSpec((nc, KV_WIDTH), lambda i: (0, 0))],
        out_shape=[jax.ShapeDtypeStruct((nc, KV_WIDTH), BF16), jax.ShapeDtypeStruct((nc, KV_WIDTH), BF16)],
        scratch_shapes=[pltpu.VMEM((nc, KV_WIDTH), F32), pltpu.VMEM((nc, KV_WIDTH), F32)],
        compiler_params=_cp("arbitrary"), name="nsa_compress",
    )(k_f32, v_f32, cos_c, sin_c)


NSA_TQ = 128
NSA_TK = 512


def _softmax_rows(s):
    mx = jnp.max(s, axis=-1, keepdims=True)
    p = jnp.exp(s - mx)
    return p / jnp.sum(p, axis=-1, keepdims=True)


def _nsa_kernel(q_ref, kcb_ref, vcb_ref, ks_ref, vs_ref, kw_ref, vw_ref, g_ref, m_ref, e_ref, o_ref, v_scr, *, nc, nsp):
    b = pl.program_id(1)
    tq, g4 = NSA_TQ, B_GROUP
    q = q_ref[...]
    qs = jnp.concatenate([q[:, g * LANES:(g + 1) * LANES] for g in range(g4)], axis=0)
    pos = b * tq + lax.broadcasted_iota(I32, (tq, 1), 0)

    sc = _dot_nt(qs, kcb_ref[...]).reshape(g4, tq, nc)
    cend = CMP_STRIDE * lax.broadcasted_iota(I32, (1, nc), 1) + (CMP_LEN - 1)
    mc = cend <= pos
    pc = _softmax_rows(jnp.where(mc[None], sc, NEG))
    pc = pc * jnp.where(pos >= CMP_LEN - 1, 1.0, 0.0)[None]
    oc = _dot(pc.reshape(g4 * tq, nc).astype(BF16), vcb_ref[...])

    pcs = pc[0] + pc[1] + pc[2] + pc[3]
    hi, mid, lo = _split3(pcs)
    mm = m_ref[...]
    imp = _dot(hi, mm) + _dot(mid, mm) + _dot(lo, mm)
    imp_t = imp.T
    n_io = lax.broadcasted_iota(I32, (nsp, tq), 0)
    cur_t = lax.shift_right_logical(b * tq + lax.broadcasted_iota(I32, (nsp, tq), 1), 6)
    val = jnp.where(n_io > cur_t, NEG, imp_t)
    forced = jnp.where(n_io == 0, 1, jnp.where(n_io == cur_t, 1, jnp.where(n_io == cur_t - 1, 1, 0)))
    val = jnp.where(forced == 1, BIG, val)
    v_scr[...] = val

    def rank_body(n2, cnt):
        r = v_scr[pl.ds(n2, 1), :]
        beats = jnp.where(r > val, 1, jnp.where(r == val, jnp.where(n2 < n_io, 1, 0), 0))
        return cnt + beats

    cnt = lax.fori_loop(0, 2 * b + 2, rank_body, jnp.zeros((nsp, tq), I32))
    sel_t = jnp.where(cnt < SEL_TOP, jnp.where(n_io <= cur_t, 1.0, 0.0), 0.0)
    sel = sel_t.T.astype(BF16)

    def sel_body(t, carry):
        m, l, acc = carry
        off = pl.multiple_of(t * NSA_TK, NSA_TK)
        kt = ks_ref[pl.ds(off, NSA_TK), :]
        vt = vs_ref[pl.ds(off, NSA_TK), :]
        s = _dot_nt(qs, kt).reshape(g4, tq, NSA_TK)
        allowed = _dot(sel, e_ref[t])
        kpos = off + lax.broadcasted_iota(I32, (tq, NSA_TK), 1)
        ok = jnp.where(kpos <= pos, allowed, 0.0) > 0.5
        s = jnp.where(ok[None], s, NEG)
        mn = jnp.maximum(m, jnp.max(s, axis=-1, keepdims=True))
        al = jnp.exp(m - mn)
        p = jnp.exp(s - mn)
        l = al * l + jnp.sum(p, axis=-1, keepdims=True)
        pv = _dot(p.reshape(g4 * tq, NSA_TK).astype(BF16), vt).reshape(g4, tq, HEAD_DIM)
        return mn, l, al * acc + pv

    ntiles = (b * tq + tq + NSA_TK - 1) // NSA_TK
    m0 = jnp.full((g4, tq, 1), NEG, F32)
    l0 = jnp.zeros((g4, tq, 1), F32)
    a0 = jnp.zeros((g4, tq, HEAD_DIM), F32)
    _, ls, accs = lax.fori_loop(0, ntiles, sel_body, (m0, l0, a0))
    osel = accs / ls

    wlen = WINDOW + tq
    wstart = pl.multiple_of(jnp.maximum(b * tq - WINDOW, 0), tq)
    kwt = kw_ref[pl.ds(wstart, wlen), :]
    vwt = vw_ref[pl.ds(wstart, wlen), :]
    sw = _dot_nt(qs, kwt).reshape(g4, tq, wlen)
    kp = wstart + lax.broadcasted_iota(I32, (tq, wlen), 1)
    okw = jnp.where(kp <= pos, jnp.where(kp > pos - WINDOW, 1, 0), 0) == 1
    pw = _softmax_rows(jnp.where(okw[None], sw, NEG))
    ow = _dot(pw.reshape(g4 * tq, wlen).astype(BF16), vwt).reshape(g4, tq, HEAD_DIM)

    oc = oc.reshape(g4, tq, HEAD_DIM)
    gt = g_ref[...]
    for g in range(g4):
        og = gt[:, 3 * g:3 * g + 1] * oc[g] + gt[:, 3 * g + 1:3 * g + 2] * osel[g] + gt[:, 3 * g + 2:3 * g + 3] * ow[g]
        o_ref[:, g * LANES:(g + 1) * LANES] = og.astype(o_ref.dtype)


def _nsa_tables(s):
    nc = s // CMP_STRIDE
    ns = s // SEL_LEN
    nsp = -(-ns // LANES) * LANES
    j = jnp.arange(nc)[:, None]
    n = jnp.arange(nsp)[None, :]
    pool = ((j >= 4 * n - 1) & (j <= 4 * n + 3) & (n < ns) & (j < nc - 1)).astype(BF16)
    t = jnp.arange(s)[None, :]
    expand = ((t // SEL_LEN) == jnp.arange(nsp)[:, None]).astype(BF16)
    expand = expand.reshape(nsp, s // NSA_TK, NSA_TK).transpose(1, 0, 2)
    return pool, expand


def _nsa_prompt(q_bf, kcb, vcb, k_bf, v_bf, w_bf, gates, pool, expand):
    s = q_bf.shape[0]
    nc = s // CMP_STRIDE
    nsp = pool.shape[1]
    nt = s // NSA_TK
    full = lambda col: pl.BlockSpec((s, HEAD_DIM), col)
    return pl.pallas_call(
        functools.partial(_nsa_kernel, nc=nc, nsp=nsp),
        grid=(B_KV, s // NSA_TQ),
        in_specs=[pl.BlockSpec((NSA_TQ, B_GROUP * HEAD_DIM), lambda k, b: (b, 1 + k)),
                  pl.BlockSpec((nc, HEAD_DIM), lambda k, b: (0, k)),
                  pl.BlockSpec((nc, HEAD_DIM), lambda k, b: (0, k)),
                  full(lambda k, b: (0, 6 + k)), full(lambda k, b: (0, 6 + k)),
                  full(lambda k, b: (0, k)), full(lambda k, b: (0, 2 + k)),
                  pl.BlockSpec((NSA_TQ, LANES), lambda k, b: (b, k)),
                  pl.BlockSpec((nc, nsp), lambda k, b: (0, 0)),
                  pl.BlockSpec((nt, nsp, NSA_TK), lambda k, b: (0, 0, 0))],
        out_specs=pl.BlockSpec((NSA_TQ, B_GROUP * HEAD_DIM), lambda k, b: (b, k)),
        out_shape=jax.ShapeDtypeStruct((s, B_WIDTH), BF16),
        scratch_shapes=[pltpu.VMEM((nsp, NSA_TQ), F32)],
        compiler_params=_cp("parallel", "arbitrary"), name="nsa",
    )(q_bf, kcb, vcb, k_bf, v_bf, w_bf, w_bf, gates, pool, expand)


def _pool_kernel(u_ref, up_ref, pw_ref, ps_ref, o_ref, ext, *, tm):
    i = pl.program_id(0)
    cur = u_ref[...]
    prev = jnp.where(i > 0, up_ref[...], 0.0)
    ext[0:16, :] = prev
    ext[16:16 + tm, :] = cur
    pos = i * tm + lax.broadcasted_iota(I32, (tm, 1), 0)
    for g, w in enumerate(C_WINDOWS):
        cols = slice(g * C_GW, (g + 1) * C_GW)
        acc = cur[:, cols]
        for r in range(1, w):
            acc = acc + ext[16 - r:16 - r + tm, cols]
        cnt = jnp.minimum(pos + 1, w).astype(F32)
        d = acc / cnt - cur[:, cols]
        o = _dot(d.astype(BF16), pw_ref[g]) * ps_ref[:, cols]
        o_ref[:, cols] = o.astype(o_ref.dtype)


def _pool_prompt(u, pool_w_bf, pool_scale, tm=512):
    s = u.shape[0]
    tm = min(tm, s)
    return pl.pallas_call(
        functools.partial(_pool_kernel, tm=tm),
        grid=(s // tm,),
        in_specs=[pl.BlockSpec((tm, C_WIDTH), lambda i: (i, 0)),
                  pl.BlockSpec((16, C_WIDTH), lambda i: (jnp.maximum(i * (tm // 16) - 1, 0), 0)),
                  pl.BlockSpec((4, C_GW, C_GW), lambda i: (0, 0, 0)),
                  pl.BlockSpec((1, C_WIDTH), lambda i: (0, 0))],
        out_specs=pl.BlockSpec((tm, C_WIDTH), lambda i: (i, 0)),
        out_shape=jax.ShapeDtypeStruct((s, C_WIDTH), BF16),
        scratch_shapes=[pltpu.VMEM((tm + 16, C_WIDTH), F32)],
        compiler_params=_cp("parallel"), name="pool_mixer",
    )(u, u, pool_w_bf, pool_scale.reshape(1, C_WIDTH))


def _ln(y, g, b):
    mu = jnp.mean(y, axis=-1, keepdims=True)
    yc = y - mu
    var = jnp.mean(yc * yc, axis=-1, keepdims=True)
    return yc * lax.rsqrt(var + LN_EPS) * g + b


def _outproj_kernel(oa_ref, ob_ref, oc_ref, x_ref, w_ref, g_ref, b_ref, of_ref, ob16_ref, *, alpha):
    h = _dot(oa_ref[...], w_ref[0:A_WIDTH, :])
    h = h + _dot(ob_ref[...], w_ref[A_WIDTH:A_WIDTH + B_WIDTH, :])
    h = h + _dot(oc_ref[...], w_ref[A_WIDTH + B_WIDTH:, :])
    y = _ln(alpha * x_ref[...] + h, g_ref[...], b_ref[...])
    of_ref[...] = y
    ob16_ref[...] = y.astype(BF16)


def _outproj_ln(oa, ob, oc, x, w_bf, g, b, alpha, tm=256):
    m, d = x.shape
    tm = min(tm, m)
    row = lambda n: pl.BlockSpec((tm, n), lambda i: (i, 0))
    vec = pl.BlockSpec((1, d), lambda i: (0, 0))
    return pl.pallas_call(
        functools.partial(_outproj_kernel, alpha=alpha),
        grid=(m // tm,),
        in_specs=[row(A_WIDTH), row(B_WIDTH), row(C_WIDTH), row(d), pl.BlockSpec((d, d), lambda i: (0, 0)), vec, vec],
        out_specs=[row(d), row(d)],
        out_shape=[jax.ShapeDtypeStruct((m, d), F32), jax.ShapeDtypeStruct((m, d), BF16)],
        compiler_params=_cp("parallel"), name="outproj_ln",
    )(oa, ob, oc, x, w_bf, g.reshape(1, d), b.reshape(1, d))


def _ffn_kernel(xb_ref, x_ref, gate_ref, wg_ref, wu_ref, wd_ref, g_ref, b_ref, of_ref, ob16_ref, acc, *, alpha, gated):
    e, f = pl.program_id(1), pl.program_id(2)

    @pl.when((e == 0) & (f == 0))
    def _():
        acc[...] = jnp.zeros_like(acc)

    xb = xb_ref[...]
    hg = _dot(xb, wg_ref[...])
    hu = _dot(xb, wu_ref[...])
    h = hg * (1.0 / (1.0 + jnp.exp(-hg))) * hu
    y = _dot(h.astype(BF16), wd_ref[...])
    if gated:
        y = gate_ref[...] * y
    acc[...] += y

    @pl.when((e == pl.num_programs(1) - 1) & (f == pl.num_programs(2) - 1))
    def _():
        o = _ln(alpha * x_ref[...] + acc[...], g_ref[...], b_ref[...])
        of_ref[...] = o
        ob16_ref[...] = o.astype(BF16)


def _ffn_ln(xb, x, gate3, wg, wu, wd, g, b, alpha, gated, tm=512, tf=256):
    m, d = x.shape
    ne, _, ff = wg.shape
    tm = min(tm, m)
    vec = pl.BlockSpec((1, d), lambda i, e, f: (0, 0))
    return pl.pallas_call(
        functools.partial(_ffn_kernel, alpha=alpha, gated=gated),
        grid=(m // tm, ne, ff // tf),
        in_specs=[pl.BlockSpec((tm, d), lambda i, e, f: (i, 0)),
                  pl.BlockSpec((tm, d), lambda i, e, f: (i, 0)),
                  pl.BlockSpec((None, tm, 1), lambda i, e, f: (e, i, 0)),
                  pl.BlockSpec((None, d, tf), lambda i, e, f: (e, 0, f)),
                  pl.BlockSpec((None, d, tf), lambda i, e, f: (e, 0, f)),
                  pl.BlockSpec((None, tf, d), lambda i, e, f: (e, f, 0)),
                  vec, vec],
        out_specs=[pl.BlockSpec((tm, d), lambda i, e, f: (i, 0)), pl.BlockSpec((tm, d), lambda i, e, f: (i, 0))],
        out_shape=[jax.ShapeDtypeStruct((m, d), F32), jax.ShapeDtypeStruct((m, d), BF16)],
        scratch_shapes=[pltpu.VMEM((tm, d), F32)],
        compiler_params=_cp("parallel", "arbitrary", "arbitrary"), name="ffn_ln",
    )(xb, x, gate3, wg, wu, wd, g.reshape(1, d), b.reshape(1, d))


PROJ_SIZES = (A_WIDTH, A_WIDTH, A_WIDTH, B_WIDTH, KV_WIDTH, KV_WIDTH, KV_WIDTH, KV_WIDTH, KV_WIDTH, KV_WIDTH, 3 * B_HEADS, C_WIDTH)
K_KINDS = ("r64",) * 4 + ("none",) * 2 + ("r128",) * 2
V_KINDS = ("none",) * 8
W_KINDS = ("r128",) * 2 + ("none",) * 2
Q_KINDS = ("r64",) * 4 + ("r128",) * 8
Q_SCALES = (A_HALF ** -0.5,) * 4 + (HEAD_DIM ** -0.5,) * 8


def _split_w_in(w):
    parts, o = [], 0
    for n in PROJ_SIZES:
        parts.append(w[:, o:o + n])
        o += n
    qa, ka, va, qb, kc, vc, ks, vs, kw, vw, gb, u = parts
    d = w.shape[0]
    per_kv = 3 * B_GROUP
    zpad = jnp.zeros((d, LANES - per_kv), w.dtype)
    wg = jnp.concatenate([gb[:, :per_kv], zpad, gb[:, per_kv:], zpad], axis=1)
    cat = lambda *xs: jnp.concatenate(xs, axis=1).astype(BF16)
    return dict(k=cat(ka, kc, ks), v=cat(va, vc, vs), w=cat(kw, vw), q=cat(qa, qb), u=u.astype(BF16), g=wg.astype(BF16))


def _project_all(x_bf, w, tabs, tm=512):
    kf, kb = _proj(x_bf, w["k"], tabs, K_KINDS, tm=tm)
    vf, vb = _proj(x_bf, w["v"], tabs, V_KINDS, tm=tm)
    wf, wb = _proj(x_bf, w["w"], tabs, W_KINDS, tm=tm)
    (qb,) = _proj(x_bf, w["q"], tabs, Q_KINDS, scales=Q_SCALES, want_f32=False, tm=tm)
    (uf,) = _proj(x_bf, w["u"], tabs, ("none",) * 4, want_bf16=False, tm=tm)
    (gf,) = _proj(x_bf, w["g"], tabs, ("sig",) * 2, want_bf16=False, tm=tm)
    return kf, kb, vf, vb, wf, wb, qb, uf, gf


def _prompt_mixer(x, x_bf, w, w_out_bf, lam_vec, lam_init, norm_g, pool_w_bf, pool_scale, ln_g, ln_b, alpha, consts):
    s = x.shape[0]
    kf, kb, vf, vb, wf, wb, qb, uf, gf = _project_all(x_bf, w, consts["tabs"])
    oa = _diff_attn_prompt(qb, kb, vb, lam_vec, norm_g, lam_init)
    kcb, vcb = _compress_prompt(kf, vf, consts["cos_c"], consts["sin_c"])
    ob = _nsa_prompt(qb, kcb, vcb, kb, vb, wb, gf, consts["pool"], consts["expand"])
    oc = _pool_prompt(uf, pool_w_bf, pool_scale)
    xn, xn_bf = _outproj_ln(oa, ob, oc, x, w_out_bf, ln_g, ln_b, alpha)
    wbuf = min(WINDOW, s)
    return xn, xn_bf, kf, vf, wf[s - wbuf:], uf[s - POOL_HIST:]


def _prompt_consts(s):
    pos = jnp.arange(s, dtype=I32)
    tabs = _rope_tables(pos)
    nc = s // CMP_STRIDE
    cend = jnp.arange(nc, dtype=I32) * CMP_STRIDE + (CMP_LEN - 1)
    cos_c, sin_c, _, _ = _rope_tables(cend)
    pool, expand = _nsa_tables(s)
    return dict(tabs=tabs, cos_c=cos_c, sin_c=sin_c, pool=pool, expand=expand)


def _router_kernel(x_ref, rw_ref, gate_ref, idx_ref, gv_ref):
    xh, xm, xl = _split3(x_ref[...])
    wh, wm, wl = _split3(rw_ref[...])
    logits = _dot(xh, wh) + (_dot(xh, wm) + _dot(xm, wh)) + (_dot(xh, wl) + _dot(xm, wm) + _dot(xl, wh))
    lane = lax.broadcasted_iota(I32, logits.shape, 1).astype(F32)
    ninf = -jnp.inf
    lg = jnp.where(lane < N_EXPERTS, logits, ninf)
    m1 = jnp.max(lg, axis=-1, keepdims=True)
    i1 = jnp.min(jnp.where(lg == m1, lane, float(LANES)), axis=-1, keepdims=True)
    lg2 = jnp.where(lane == i1, ninf, lg)
    m2 = jnp.max(lg2, axis=-1, keepdims=True)
    i2 = jnp.min(jnp.where(lg2 == m2, lane, float(LANES)), axis=-1, keepdims=True)
    e2 = jnp.exp(m2 - m1)
    den = 1.0 + e2
    g1 = 1.0 / den
    g2 = e2 / den
    gate_ref[...] = jnp.where(lane == i1, g1, 0.0) + jnp.where(lane == i2, g2, 0.0)
    idx_ref[...] = jnp.where(lane == 0.0, i1, jnp.where(lane == 1.0, i2, 0.0)).astype(I32)
    gv_ref[...] = jnp.where(lane == 0.0, g1, jnp.where(lane == 1.0, g2, 0.0))


def _router(x, router_w, tm=512):
    m, d = x.shape
    tm = min(tm, m)
    rw = jnp.concatenate([router_w, jnp.zeros((d, LANES - N_EXPERTS), F32)], axis=1)
    blk = pl.BlockSpec((tm, LANES), lambda i: (i, 0))
    return pl.pallas_call(
        _router_kernel, grid=(m // tm,),
        in_specs=[pl.BlockSpec((tm, d), lambda i: (i, 0)), pl.BlockSpec((d, LANES), lambda i: (0, 0))],
        out_specs=[blk, blk, blk],
        out_shape=[jax.ShapeDtypeStruct((m, LANES), F32), jax.ShapeDtypeStruct((m, LANES), I32), jax.ShapeDtypeStruct((m, LANES), F32)],
        compiler_params=_cp("parallel"), name="router",
    )(x, rw)


MOE_TM = 512


def _route_tables(ids, gvals, tm):
    s = ids.shape[0]
    na = 2 * s
    flat_e = ids.reshape(-1)
    onehot = (flat_e[:, None] == jnp.arange(N_EXPERTS, dtype=I32)[None, :]).astype(I32)
    csum = jnp.cumsum(onehot, axis=0)
    rank = jnp.take_along_axis(csum, flat_e[:, None], axis=1)[:, 0] - 1
    counts = csum[-1]
    pcount = ((counts + tm - 1) // tm) * tm
    ends = jnp.cumsum(pcount)
    offs = ends - pcount
    dest = (offs[flat_e] + rank).astype(I32)
    nrows = na + N_EXPERTS * tm
    ntiles = nrows // tm
    src = jnp.zeros((nrows,), I32).at[dest].set(jnp.arange(na, dtype=I32) // 2)
    gsort = jnp.zeros((nrows,), F32).at[dest].set(gvals.reshape(-1))
    tstart = jnp.arange(ntiles, dtype=I32) * tm
    te = jnp.sum((tstart[:, None] >= ends[None, :]).astype(I32), axis=1)
    valid = (te < N_EXPERTS).astype(I32)
    last_e = jnp.max(jnp.where(pcount > 0, jnp.arange(N_EXPERTS, dtype=I32), 0))
    te = jnp.where(valid == 1, te, last_e).astype(I32)
    return src, gsort.reshape(nrows, 1), te, valid, dest


def _row_copy(src_hbm, row, dst, r, sem):
    return pltpu.make_async_copy(src_hbm.at[pl.ds(row, 1), :], dst.at[pl.ds(r, 1), :], sem)


def _moe_kernel(src_ref, te_ref, valid_ref, x_hbm, gs_ref, wg_ref, wu_ref, wd_ref, y_ref, xbuf, xb, acc, sem, *, tm):
    i, f = pl.program_id(0), pl.program_id(1)

    @pl.when(f == 0)
    def _():
        base = i * tm

        def issue(r, c):
            _row_copy(x_hbm, src_ref[base + r], xbuf, r, sem).start()
            return c

        def wait(r, c):
            _row_copy(x_hbm, 0, xbuf, r, sem).wait()
            return c

        lax.fori_loop(0, tm, issue, 0)
        lax.fori_loop(0, tm, wait, 0)
        xb[...] = xbuf[...].astype(BF16)
        acc[...] = jnp.zeros_like(acc)

    @pl.when(valid_ref[i] == 1)
    def _():
        x = xb[...]
        hg = _dot(x, wg_ref[...])
        hu = _dot(x, wu_ref[...])
        h = hg * (1.0 / (1.0 + jnp.exp(-hg))) * hu
        acc[...] += _dot(h.astype(BF16), wd_ref[...])

    @pl.when(f == pl.num_programs(1) - 1)
    def _():
        y_ref[...] = gs_ref[...] * acc[...]


def _moe_grouped(x, src, gsort, te, valid, wg, wu, wd, tm, tf=256):
    d = x.shape[1]
    ff = wg.shape[2]
    nrows = gsort.shape[0]
    grid_spec = pltpu.PrefetchScalarGridSpec(
        num_scalar_prefetch=3, grid=(nrows // tm, ff // tf),
        in_specs=[pl.BlockSpec(memory_space=pl.ANY),
                  pl.BlockSpec((tm, 1), lambda i, f, s_, e_, v_: (i, 0)),
                  pl.BlockSpec((None, d, tf), lambda i, f, s_, e_, v_: (e_[i], 0, f)),
                  pl.BlockSpec((None, d, tf), lambda i, f, s_, e_, v_: (e_[i], 0, f)),
                  pl.BlockSpec((None, tf, d), lambda i, f, s_, e_, v_: (e_[i], f, 0))],
        out_specs=pl.BlockSpec((tm, d), lambda i, f, s_, e_, v_: (i, 0)),
        scratch_shapes=[pltpu.VMEM((tm, d), F32), pltpu.VMEM((tm, d), BF16), pltpu.VMEM((tm, d), F32),
                        pltpu.SemaphoreType.DMA(())])
    return pl.pallas_call(
        functools.partial(_moe_kernel, tm=tm), grid_spec=grid_spec,
        out_shape=jax.ShapeDtypeStruct((nrows, d), F32),
        compiler_params=_cp("arbitrary", "arbitrary"), name="moe_grouped",
    )(src, te, valid, x, gsort, wg, wu, wd)


def _combine_kernel(dest_ref, y_hbm, x_ref, g_ref, b_ref, of_ref, ob16_ref, buf, sem, *, tm, alpha):
    base = 2 * pl.program_id(0) * tm

    def issue(t, c):
        _row_copy(y_hbm, dest_ref[base + 2 * t], buf.at[0], t, sem).start()
        _row_copy(y_hbm, dest_ref[base + 2 * t + 1], buf.at[1], t, sem).start()
        return c

    def wait(t, c):
        _row_copy(y_hbm, 0, buf.at[0], t, sem).wait()
        _row_copy(y_hbm, 0, buf.at[1], t, sem).wait()
        return c

    lax.fori_loop(0, tm, issue, 0)
    lax.fori_loop(0, tm, wait, 0)
    o = _ln(alpha * x_ref[...] + (buf[0] + buf[1]), g_ref[...], b_ref[...])
    of_ref[...] = o
    ob16_ref[...] = o.astype(BF16)


def _moe_combine_ln(y_sorted, dest, x, g, b, alpha, tm=256):
    m, d = x.shape
    row = pl.BlockSpec((tm, d), lambda i, d_: (i, 0))
    vec = pl.BlockSpec((1, d), lambda i, d_: (0, 0))
    grid_spec = pltpu.PrefetchScalarGridSpec(
        num_scalar_prefetch=1, grid=(m // tm,),
        in_specs=[pl.BlockSpec(memory_space=pl.ANY), row, vec, vec],
        out_specs=[row, row],
        scratch_shapes=[pltpu.VMEM((2, tm, d), F32), pltpu.SemaphoreType.DMA(())])
    return pl.pallas_call(
        functools.partial(_combine_kernel, tm=tm, alpha=alpha), grid_spec=grid_spec,
        out_shape=[jax.ShapeDtypeStruct((m, d), F32), jax.ShapeDtypeStruct((m, d), BF16)],
        compiler_params=_cp("arbitrary"), name="moe_combine_ln",
    )(dest, y_sorted, x, g.reshape(1, d), b.reshape(1, d))


def _moe_prompt(xn, router_w, wg, wu, wd, g, b, alpha):
    _, idx, gv = _router(xn, router_w)
    src, gsort, te, valid, dest = _route_tables(idx[:, :TOP_K], gv[:, :TOP_K], MOE_TM)
    y_sorted = _moe_grouped(xn, src, gsort, te, valid, wg, wu, wd, MOE_TM)
    return _moe_combine_ln(y_sorted, dest, xn, g, b, alpha)


def _head_rows(row, width=HEAD_DIM, reps=2, heads=A_HEADS, first=0):
    parts = []
    for h in range(heads):
        parts += [row[:, (first + h) * width:(first + h + 1) * width]] * reps
    return jnp.concatenate(parts, axis=0)


def _sdiff_kernel(pt_ref, q_ref, kp_ref, vp_ref, kn_ref, vn_ref, lam_ref, g_ref, o_ref, kch_ref, vch_ref, qm, m_s, l_s, acc, *, lam_init):
    p = pl.program_id(1)
    rows = lax.broadcasted_iota(I32, (2 * A_HEADS, HEAD_DIM), 0)
    lane = lax.broadcasted_iota(I32, (2 * A_HEADS, HEAD_DIM), 1)

    @pl.when(p == 0)
    def _():
        q8 = _head_rows(q_ref[...].astype(F32))
        keep = (lane // A_HALF) == (rows % 2)
        qm[...] = jnp.where(keep, q8, 0.0)
        m_s[...] = jnp.full_like(m_s, NEG)
        l_s[...] = jnp.zeros_like(l_s)
        acc[...] = jnp.zeros_like(acc)

    q8 = qm[...]
    kp = kp_ref[...]
    vp = vp_ref[...]
    s = jnp.zeros((2 * A_HEADS, PAGE_SIZE), F32)
    for h in range(A_HEADS):
        qh = jnp.where(rows // 2 == h, q8, 0.0).astype(BF16)
        s = s + _dot_nt(qh, kp[:, h * HEAD_DIM:(h + 1) * HEAD_DIM].astype(BF16))
    mn = jnp.maximum(m_s[...], jnp.max(s, axis=-1, keepdims=True))
    al = jnp.exp(m_s[...] - mn)
    pr = jnp.exp(s - mn)
    l_s[...] = al * l_s[...] + jnp.sum(pr, axis=-1, keepdims=True)
    prow = lax.broadcasted_iota(I32, pr.shape, 0)
    pv = jnp.zeros((2 * A_HEADS, HEAD_DIM), F32)
    for h in range(A_HEADS):
        ph = jnp.where(prow // 2 == h, pr, 0.0).astype(BF16)
        pv = pv + _dot(ph, vp[:, h * HEAD_DIM:(h + 1) * HEAD_DIM].astype(BF16))
    acc[...] = al * acc[...] + pv
    m_s[...] = mn

    nchunk = PAGE_SIZE // CMP_STRIDE
    kch_ref[...] = jnp.sum(kp[:, A_WIDTH:A_WIDTH + KV_WIDTH].reshape(nchunk, CMP_STRIDE, KV_WIDTH), axis=1)
    vch_ref[...] = jnp.sum(vp[:, A_WIDTH:A_WIDTH + KV_WIDTH].reshape(nchunk, CMP_STRIDE, KV_WIDTH), axis=1)

    @pl.when(p == pl.num_programs(1) - 1)
    def _():
        k8 = _head_rows(kn_ref[...])
        v8 = _head_rows(vn_ref[...])
        sn = jnp.sum(q8 * k8, axis=-1, keepdims=True)
        m2 = jnp.maximum(m_s[...], sn)
        a2 = jnp.exp(m_s[...] - m2)
        pn = jnp.exp(sn - m2)
        lt = a2 * l_s[...] + pn
        o8 = (a2 * acc[...] + pn * v8) / lt
        lv = lam_ref[...]
        la = jnp.sum(lv[0:1] * lv[1:2], axis=1, keepdims=True)
        lb = jnp.sum(lv[2:3] * lv[3:4], axis=1, keepdims=True)
        lam = jnp.exp(la) - jnp.exp(lb) + lam_init
        outs = []
        for h in range(A_HEADS):
            o = o8[2 * h:2 * h + 1] - lam * o8[2 * h + 1:2 * h + 2]
            o = o * lax.rsqrt(jnp.mean(o * o, axis=-1, keepdims=True) + LN_EPS) * g_ref[...] * (1.0 - lam_init)
            outs.append(o)
        o_ref[...] = jnp.concatenate(outs, axis=1).astype(o_ref.dtype)


def _sample_diff(pt, q3, ck, cv, kn3, vn3, lam_vec, norm_g, lam_init):
    nb, npg = pt.shape
    nchunk = PAGE_SIZE // CMP_STRIDE
    used = A_WIDTH + KV_WIDTH
    grid_spec = pltpu.PrefetchScalarGridSpec(
        num_scalar_prefetch=1, grid=(nb, npg),
        in_specs=[pl.BlockSpec((None, 1, A_WIDTH), lambda b, p, t: (b, 0, 0)),
                  pl.BlockSpec((None, PAGE_SIZE, used), lambda b, p, t: (t[b, p], 0, 0)),
                  pl.BlockSpec((None, PAGE_SIZE, used), lambda b, p, t: (t[b, p], 0, 0)),
                  pl.BlockSpec((None, 1, A_WIDTH), lambda b, p, t: (b, 0, 0)),
                  pl.BlockSpec((None, 1, A_WIDTH), lambda b, p, t: (b, 0, 0)),
                  pl.BlockSpec((4, A_HALF), lambda b, p, t: (0, 0)),
                  pl.BlockSpec((1, HEAD_DIM), lambda b, p, t: (0, 0))],
        out_specs=[pl.BlockSpec((None, 1, A_WIDTH), lambda b, p, t: (b, 0, 0)),
                   pl.BlockSpec((None, nchunk, KV_WIDTH), lambda b, p, t: (b, p, 0)),
                   pl.BlockSpec((None, nchunk, KV_WIDTH), lambda b, p, t: (b, p, 0))],
        scratch_shapes=[pltpu.VMEM((2 * A_HEADS, HEAD_DIM), F32), pltpu.VMEM((2 * A_HEADS, 1), F32),
                        pltpu.VMEM((2 * A_HEADS, 1), F32), pltpu.VMEM((2 * A_HEADS, HEAD_DIM), F32)])
    return pl.pallas_call(
        functools.partial(_sdiff_kernel, lam_init=lam_init), grid_spec=grid_spec,
        out_shape=[jax.ShapeDtypeStruct((nb, 1, A_WIDTH), BF16),
                   jax.ShapeDtypeStruct((nb, npg * nchunk, KV_WIDTH), F32),
                   jax.ShapeDtypeStruct((nb, npg * nchunk, KV_WIDTH), F32)],
        compiler_params=_cp("parallel", "arbitrary"), name="sample_diff",
    )(pt, q3, ck, cv, kn3, vn3, lam_vec, norm_g.reshape(1, HEAD_DIM))


def _group_rows(q_row):
    q4 = jnp.concatenate([q_row[:, g * HEAD_DIM:(g + 1) * HEAD_DIM] for g in range(B_GROUP)], axis=0)
    return jnp.concatenate([q4, jnp.zeros_like(q4)], axis=0)


def _snsa_kernel(q0_ref, q1_ref, kch_ref, vch_ref, kn_ref, vn_ref, cos_ref, sin_ref, m_ref, st_ref, wn_ref, g_ref,
                 idx_ref, ocw_ref, nw_ref, *, pos, nc, nsp):
    rowc = lax.broadcasted_iota(I32, (nc, KV_WIDTH), 0)
    kn = kn_ref[...]
    vn = vn_ref[...]
    c0 = A_WIDTH

    def blocks(ch, new_row):
        nxt = jnp.where(rowc == nc - 1, new_row, pltpu.roll(ch, nc - 1, 0))
        return (ch + nxt) / CMP_LEN

    kb = blocks(kch_ref[...], kn[:, c0:c0 + KV_WIDTH])
    vcb = blocks(vch_ref[...], vn[:, c0:c0 + KV_WIDTH]).astype(BF16)
    kcb = []
    for c in range(B_KV):
        blk = kb[:, c * LANES:(c + 1) * LANES]
        kcb.append((blk * cos_ref[...] + pltpu.roll(blk, 64, 1) * sin_ref[...]).astype(BF16))

    cend = CMP_STRIDE * lax.broadcasted_iota(I32, (8, nc), 1) + (CMP_LEN - 1)
    mc = cend <= pos
    any_valid = 1.0 if pos >= CMP_LEN - 1 else 0.0
    cur = pos // SEL_LEN
    st = st_ref[...]
    wb = st.shape[0]
    wn = wn_ref[...]
    gt = g_ref[...]
    wcol = lax.broadcasted_iota(I32, (8, wb), 1)
    okw = (pos - wb + wcol) > pos - WINDOW
    n_row = lax.broadcasted_iota(I32, (1, nsp), 1)
    ii = lax.broadcasted_iota(I32, (nsp, nsp), 0)
    jj = lax.broadcasted_iota(I32, (nsp, nsp), 1)
    rr = lax.broadcasted_iota(I32, (SEL_TOP, nsp), 0).astype(F32)
    nn = lax.broadcasted_iota(I32, (SEL_TOP, nsp), 1).astype(F32)
    outs = []
    for k, q_ref in enumerate((q0_ref, q1_ref)):
        qs = _group_rows(q_ref[...])
        sc = _dot_nt(qs, kcb[k])
        pc = _softmax_rows(jnp.where(mc, sc, NEG)) * any_valid
        oc = _dot(pc.astype(BF16), vcb[:, k * LANES:(k + 1) * LANES])
        pcs = pc[0:1] + pc[1:2] + pc[2:3] + pc[3:4]
        hi, mid, lo = _split3(jnp.broadcast_to(pcs, (8, nc)))
        mm = m_ref[...]
        imp = (_dot(hi, mm) + _dot(mid, mm) + _dot(lo, mm))[0:1]
        val = jnp.where(n_row > cur, NEG, imp)
        forced = jnp.where(n_row == 0, 1, jnp.where(n_row == cur, 1, jnp.where(n_row == cur - 1, 1, 0)))
        val = jnp.where(forced == 1, BIG, val)
        a_m = jnp.broadcast_to(val, (nsp, nsp))
        b_m = a_m.T
        beats = jnp.where(b_m > a_m, 1.0, jnp.where(b_m == a_m, jnp.where(ii < jj, 1.0, 0.0), 0.0))
        rank = jnp.sum(beats, axis=0, keepdims=True)
        hit = jnp.broadcast_to(rank, (SEL_TOP, nsp)) == rr
        picked = jnp.sum(jnp.where(hit, nn, 0.0), axis=1, keepdims=True)
        idx_ref[k] = jnp.broadcast_to(picked, (SEL_TOP, LANES)).astype(I32)

        kw_s = st[:, k * LANES:(k + 1) * LANES].astype(BF16)
        vw_s = st[:, (B_KV + k) * LANES:(B_KV + k + 1) * LANES].astype(BF16)
        sw = jnp.where(okw, _dot_nt(qs, kw_s), NEG)
        qf = qs.astype(F32)
        s_new = jnp.sum(qf * wn[:, k * LANES:(k + 1) * LANES], axis=-1, keepdims=True)
        mw = jnp.maximum(jnp.max(sw, axis=-1, keepdims=True), s_new)
        pw = jnp.exp(sw - mw)
        pn = jnp.exp(s_new - mw)
        ow = (_dot(pw.astype(BF16), vw_s) + pn * wn[:, (B_KV + k) * LANES:(B_KV + k + 1) * LANES]) / (jnp.sum(pw, axis=-1, keepdims=True) + pn)
        for g in range(B_GROUP):
            c = k * LANES + 3 * g
            outs.append(gt[:, c:c + 1] * oc[g:g + 1] + gt[:, c + 2:c + 3] * ow[g:g + 1])
    ocw_ref[...] = jnp.concatenate(outs, axis=1)
    roww = lax.broadcasted_iota(I32, st.shape, 0)
    nw_ref[...] = jnp.where(roww == wb - 1, wn, pltpu.roll(st, wb - 1, 0))


def _sample_nsa_a(q3, kch, vch, kn3, vn3, cos_c, sin_c, pool, st, wn3, g3, pos):
    nb = q3.shape[0]
    nc = kch.shape[1]
    nsp = pool.shape[1]
    wb = st.shape[1]
    b3 = lambda n, c=0: pl.BlockSpec((None, 1, n), lambda b: (b, 0, c))
    return pl.pallas_call(
        functools.partial(_snsa_kernel, pos=pos, nc=nc, nsp=nsp),
        grid=(nb,),
        in_specs=[b3(4 * HEAD_DIM, 1), b3(4 * HEAD_DIM, 2),
                  pl.BlockSpec((None, nc, KV_WIDTH), lambda b: (b, 0, 0)),
                  pl.BlockSpec((None, nc, KV_WIDTH), lambda b: (b, 0, 0)),
                  b3(8 * HEAD_DIM), b3(8 * HEAD_DIM),
                  pl.BlockSpec((nc, LANES), lambda b: (0, 0)), pl.BlockSpec((nc, LANES), lambda b: (0, 0)),
                  pl.BlockSpec((nc, nsp), lambda b: (0, 0)),
                  pl.BlockSpec((None, wb, 4 * HEAD_DIM), lambda b: (b, 0, 0)),
                  b3(4 * HEAD_DIM), b3(2 * LANES)],
        out_specs=[pl.BlockSpec((None, B_KV, SEL_TOP, LANES), lambda b: (b, 0, 0, 0)),
                   b3(B_WIDTH),
                   pl.BlockSpec((None, wb, 4 * HEAD_DIM), lambda b: (b, 0, 0))],
        out_shape=[jax.ShapeDtypeStruct((nb, B_KV, SEL_TOP, LANES), I32),
                   jax.ShapeDtypeStruct((nb, 1, B_WIDTH), F32),
                   jax.ShapeDtypeStruct((nb, wb, 4 * HEAD_DIM), F32)],
        compiler_params=_cp("parallel"), name="sample_nsa_a",
    )(q3, q3, kch, vch, kn3, vn3, cos_c, sin_c, pool, st, wn3, g3)


def _ssel_kernel(pt_ref, idx_ref, q_ref, kb_ref, vb_ref, kn_ref, vn_ref, g_ref, ocw_ref, o_ref, m_s, l_s, acc, *, nblk):
    b, k, r = pl.program_id(0), pl.program_id(1), pl.program_id(2)
    qs = _group_rows(q_ref[...])

    @pl.when(r == 0)
    def _():
        s_new = jnp.sum(qs.astype(F32) * kn_ref[...], axis=-1, keepdims=True)
        m_s[...] = s_new
        l_s[...] = jnp.ones_like(l_s)
        acc[...] = jnp.broadcast_to(vn_ref[...], acc.shape)

    valid = idx_ref[(b * B_KV + k) * SEL_TOP + r] < nblk
    s = jnp.where(valid, _dot_nt(qs, kb_ref[...].astype(BF16)), NEG)
    mn = jnp.maximum(m_s[...], jnp.max(s, axis=-1, keepdims=True))
    al = jnp.exp(m_s[...] - mn)
    p = jnp.exp(s - mn)
    l_s[...] = al * l_s[...] + jnp.sum(p, axis=-1, keepdims=True)
    acc[...] = al * acc[...] + _dot(p.astype(BF16), vb_ref[...].astype(BF16))
    m_s[...] = mn

    @pl.when(r == pl.num_programs(2) - 1)
    def _():
        osel = acc[...] / l_s[...]
        gt = g_ref[...]
        ocw = ocw_ref[...]
        outs = []
        for g in range(B_GROUP):
            outs.append(ocw[:, g * LANES:(g + 1) * LANES] + gt[:, 3 * g + 1:3 * g + 2] * osel[g:g + 1])
        o_ref[...] = jnp.concatenate(outs, axis=1).astype(o_ref.dtype)


def _sample_sel(pt, idx, q3, ck_half, cv_half, kn3, vn3, g3, ocw):
    nb = q3.shape[0]
    nblk = pt.shape[1] * (PAGE_SIZE // SEL_LEN)
    per = PAGE_SIZE // SEL_LEN

    def blk_map(b, k, r, t, ix):
        n = jnp.minimum(ix[(b * B_KV + k) * SEL_TOP + r], nblk - 1)
        return (t[b, n // per] * per + n % per, 0, 6 + k)

    grid_spec = pltpu.PrefetchScalarGridSpec(
        num_scalar_prefetch=2, grid=(nb, B_KV, SEL_TOP),
        in_specs=[pl.BlockSpec((None, 1, 4 * HEAD_DIM), lambda b, k, r, t, ix: (b, 0, 1 + k)),
                  pl.BlockSpec((None, SEL_LEN, HEAD_DIM), blk_map),
                  pl.BlockSpec((None, SEL_LEN, HEAD_DIM), blk_map),
                  pl.BlockSpec((None, 1, HEAD_DIM), lambda b, k, r, t, ix: (b, 0, 6 + k)),
                  pl.BlockSpec((None, 1, HEAD_DIM), lambda b, k, r, t, ix: (b, 0, 6 + k)),
                  pl.BlockSpec((None, 1, LANES), lambda b, k, r, t, ix: (b, 0, k)),
                  pl.BlockSpec((None, 1, 4 * HEAD_DIM), lambda b, k, r, t, ix: (b, 0, k))],
        out_specs=pl.BlockSpec((None, 1, 4 * HEAD_DIM), lambda b, k, r, t, ix: (b, 0, k)),
        scratch_shapes=[pltpu.VMEM((8, 1), F32), pltpu.VMEM((8, 1), F32), pltpu.VMEM((8, HEAD_DIM), F32)])
    return pl.pallas_call(
        functools.partial(_ssel_kernel, nblk=nblk), grid_spec=grid_spec,
        out_shape=jax.ShapeDtypeStruct((nb, 1, B_WIDTH), BF16),
        compiler_params=_cp("parallel", "parallel", "arbitrary"), name="sample_sel",
    )(pt, idx, q3, ck_half, cv_half, kn3, vn3, g3, ocw)


def _spool_kernel(st_ref, u_ref, pw_ref, ps_ref, o_ref, ns_ref):
    u = u_ref[...]
    for g, w in enumerate(C_WINDOWS):
        cols = slice(g * C_GW, (g + 1) * C_GW)
        acc = u[:, cols]
        for r in range(1, w):
            o0 = (POOL_HIST - r) * C_WIDTH + g * C_GW
            acc = acc + st_ref[:, o0:o0 + C_GW]
        d = acc / float(w) - u[:, cols]
        o_ref[:, cols] = (_dot(d.astype(BF16), pw_ref[g]) * ps_ref[:, cols]).astype(o_ref.dtype)
    ns_ref[:, 0:(POOL_HIST - 1) * C_WIDTH] = st_ref[:, C_WIDTH:]
    ns_ref[:, (POOL_HIST - 1) * C_WIDTH:] = u


def _sample_pool(st2, u, pool_w_bf, pool_scale):
    nb = u.shape[0]
    return pl.pallas_call(
        _spool_kernel,
        out_shape=[jax.ShapeDtypeStruct((nb, C_WIDTH), BF16), jax.ShapeDtypeStruct(st2.shape, F32)],
        compiler_params=pltpu.CompilerParams(vmem_limit_bytes=VMEM_LIMIT), name="sample_pool",
    )(st2, u, pool_w_bf, pool_scale.reshape(1, C_WIDTH))


def _sample_mixer(x, x_bf, w, w_out_bf, lam_vec, lam_init, norm_g, pool_w_bf, pool_scale, ln_g, ln_b, alpha,
                  ck, cv, st_win, st_pool, pt, sconsts):
    nb = x.shape[0]
    npool = ck.shape[0]
    past = pt.shape[1] * PAGE_SIZE
    kf, _, vf, _, wf, _, qb, uf, gf = _project_all(x_bf, w, sconsts["tabs"], tm=nb)
    q3, kn3, vn3 = qb.reshape(nb, 1, -1), kf.reshape(nb, 1, -1), vf.reshape(nb, 1, -1)
    ck2 = ck.reshape(npool, PAGE_SIZE, CACHE_HEADS * HEAD_DIM)
    cv2 = cv.reshape(npool, PAGE_SIZE, CACHE_HEADS * HEAD_DIM)
    oa, kch, vch = _sample_diff(pt, q3, ck2, cv2, kn3, vn3, lam_vec, norm_g, lam_init)
    wb = st_win.shape[1]
    idx4, ocw, new_win = _sample_nsa_a(q3, kch, vch, kn3, vn3, sconsts["cos_c"], sconsts["sin_c"], sconsts["pool"],
                                       st_win.reshape(nb, wb, 4 * HEAD_DIM), wf.reshape(nb, 1, -1), gf.reshape(nb, 1, -1), past)
    per = PAGE_SIZE // SEL_LEN
    ob = _sample_sel(pt, idx4[:, :, :, 0].reshape(-1), q3, ck2.reshape(npool * per, SEL_LEN, -1), cv2.reshape(npool * per, SEL_LEN, -1),
                     kn3, vn3, gf.reshape(nb, 1, -1), ocw)
    oc, new_pool = _sample_pool(st_pool.reshape(nb, POOL_HIST * C_WIDTH), uf, pool_w_bf, pool_scale)
    xn, xn_bf = _outproj_ln(oa.reshape(nb, A_WIDTH), ob.reshape(nb, B_WIDTH), oc, x, w_out_bf, ln_g, ln_b, alpha)
    return xn, xn_bf, kf, vf, new_win, new_pool


def _sample_consts(nb, past):
    pos = jnp.full((nb,), past, I32)
    tabs = _rope_tables(pos)
    nc = past // CMP_STRIDE
    cend = jnp.arange(nc, dtype=I32) * CMP_STRIDE + (CMP_LEN - 1)
    cos_c, sin_c, _, _ = _rope_tables(cend)
    ns = past // SEL_LEN + 1
    nsp = -(-ns // LANES) * LANES
    j = jnp.arange(nc)[:, None]
    n = jnp.arange(nsp)[None, :]
    pool = ((j >= 4 * n - 1) & (j <= 4 * n + 3) & (n < ns)).astype(BF16)
    return dict(tabs=tabs, cos_c=cos_c, sin_c=sin_c, pool=pool)


def kernel(x_prompt, x_sample, cache_k, cache_v, state_win, state_pool, page_table, w_in, w_out, diff_lambda, diff_norm_g, pool_w, pool_scale, ln1_g, ln1_b, ln2_g, ln2_b, ffn_w_gate, ffn_w_up, ffn_w_down, moe_router, moe_w_gate, moe_w_up, moe_w_down):
    depth = w_in.shape[0]
    alpha = (2 * depth) ** 0.25
    s = x_prompt.shape[1]
    nb = x_sample.shape[0]
    past = page_table.shape[1] * PAGE_SIZE
    pconsts = _prompt_consts(s)
    sconsts = _sample_consts(nb, past)
    xp, xs = x_prompt[0], x_sample[:, 0]
    xp_bf, xs_bf = xp.astype(BF16), xs.astype(BF16)
    outs = [[] for _ in range(8)]
    for l in range(depth):
        lam_init = 0.8 - 0.6 * math.exp(-0.3 * l)
        w = _split_w_in(w_in[l])
        w_out_bf = w_out[l].astype(BF16)
        pw_bf = pool_w[l].astype(BF16)
        xp, xp_bf, kf, vf, wf, uf = _prompt_mixer(xp, xp_bf, w, w_out_bf, diff_lambda[l], lam_init, diff_norm_g[l],
                                                  pw_bf, pool_scale[l], ln1_g[l], ln1_b[l], alpha, pconsts)
        xs, xs_bf, skf, svf, swin, spool = _sample_mixer(xs, xs_bf, w, w_out_bf, diff_lambda[l], lam_init, diff_norm_g[l],
                                                         pw_bf, pool_scale[l], ln1_g[l], ln1_b[l], alpha,
                                                         cache_k[l], cache_v[l], state_win[l], state_pool[l], page_table, sconsts)
        for o, v in zip(outs, (kf, vf, wf, uf, skf, svf, swin, spool)):
            o.append(v)
        i = l // 2
        if l % 2 == 0:
            wg, wu, wd = (a[i:i + 1].astype(BF16) for a in (ffn_w_gate, ffn_w_up, ffn_w_down))
            xp, xp_bf = _ffn_ln(xp_bf, xp, jnp.ones((1, s, 1), F32), wg, wu, wd, ln2_g[l], ln2_b[l], alpha, gated=False)
            xs, xs_bf = _ffn_ln(xs_bf, xs, jnp.ones((1, nb, 1), F32), wg, wu, wd, ln2_g[l], ln2_b[l], alpha, gated=False)
        else:
            wg, wu, wd = (a[i].astype(BF16) for a in (moe_w_gate, moe_w_up, moe_w_down))
            xp, xp_bf = _moe_prompt(xp, moe_router[i], wg, wu, wd, ln2_g[l], ln2_b[l], alpha)
            gate_s, _, _ = _router(xs, moe_router[i])
            gate3 = gate_s[:, :N_EXPERTS].T.reshape(N_EXPERTS, nb, 1)
            xs, xs_bf = _ffn_ln(xs_bf, xs, gate3, wg, wu, wd, ln2_g[l], ln2_b[l], alpha, gated=True)
    pk, pv, pwn, ppl, sk, sv, sw, sp = outs
    wbuf = min(WINDOW, s)
    swb = state_win.shape[2]
    return (xp[None], xs[:, None],
            jnp.stack(pk).reshape(depth, 1, s, CACHE_HEADS, HEAD_DIM),
            jnp.stack(pv).reshape(depth, 1, s, CACHE_HEADS, HEAD_DIM),
            jnp.stack(pwn).reshape(depth, 1, wbuf, 2, B_KV, HEAD_DIM),
            jnp.stack(ppl).reshape(depth, 1, POOL_HIST, C_WIDTH),
            jnp.stack(sk).reshape(depth, nb, 1, CACHE_HEADS, HEAD_DIM),
            jnp.stack(sv).reshape(depth, nb, 1, CACHE_HEADS, HEAD_DIM),
            jnp.stack(sw).reshape(depth, nb, swb, 2, B_KV, HEAD_DIM),
            jnp.stack(sp).reshape(depth, nb, POOL_HIST, C_WIDTH))
```

```python
import functools
import math

import jax
import jax.numpy as jnp
from jax import lax
from jax.experimental import pallas as pl
from jax.experimental.pallas import tpu as pltpu

F32 = jnp.float32
BF16 = jnp.bfloat16
I32 = jnp.int32

HEAD_DIM = 128
A_HEADS = 4
A_HALF = 64
A_WIDTH = 512
B_HEADS = 8
B_KV = 2
B_GROUP = 4
B_WIDTH = 1024
KV_WIDTH = 256
CMP_STRIDE = 16
CMP_LEN = 32
SEL_LEN = 64
SEL_TOP = 16
WINDOW = 512
C_WIDTH = 512
C_WINDOWS = (2, 4, 8, 16)
C_GW = 128
POOL_HIST = 15
CACHE_HEADS = 8
PAGE_SIZE = 128
N_EXPERTS = 8
TOP_K = 2
ROPE_THETA = 10000.0
LN_EPS = 1e-5
NEG = -1e30
BIG = 1e30

LANES = 128
VMEM_LIMIT = 56 * 2 ** 20


def _cp(*sem):
    return pltpu.CompilerParams(dimension_semantics=sem, vmem_limit_bytes=VMEM_LIMIT)


def _dot(a, b):
    return jnp.dot(a, b, preferred_element_type=F32)


def _dot_nt(a, b):
    return lax.dot_general(a, b, (((1,), (1,)), ((), ())), preferred_element_type=F32)


def _split3(x):
    hi = x.astype(BF16)
    r1 = x - hi.astype(F32)
    mid = r1.astype(BF16)
    lo = (r1 - mid.astype(F32)).astype(BF16)
    return hi, mid, lo


def _rope_tables(pos):
    lane = jnp.arange(LANES)
    p = pos.astype(F32)[:, None]
    inv128 = ROPE_THETA ** (-(lane % 64).astype(F32) / 64)
    a128 = p * inv128[None, :]
    cos128 = jnp.cos(a128)
    sin128 = jnp.where(lane[None, :] < 64, -jnp.sin(a128), jnp.sin(a128))
    inv64 = ROPE_THETA ** (-(lane % 32).astype(F32) / 32)
    a64 = p * inv64[None, :]
    cos64 = jnp.cos(a64)
    sin64 = jnp.where((lane[None, :] % 64) < 32, -jnp.sin(a64), jnp.sin(a64))
    return cos128, sin128, cos64, sin64


def _proj_kernel(x_ref, w_ref, c128_ref, s128_ref, c64_ref, s64_ref, *out_refs, kinds, scales, want_f32, want_bf16):
    acc = _dot(x_ref[...], w_ref[...])
    tm = acc.shape[0]
    lane = lax.broadcasted_iota(I32, (tm, LANES), 1)
    first = (lane % 64) < 32
    refs = list(out_refs)
    f_ref = refs.pop(0) if want_f32 else None
    b_ref = refs.pop(0) if want_bf16 else None
    for c, kind in enumerate(kinds):
        blk = acc[:, c * LANES:(c + 1) * LANES]
        if kind == "r128":
            blk = blk * c128_ref[...] + pltpu.roll(blk, 64, 1) * s128_ref[...]
        elif kind == "r64":
            partner = jnp.where(first, pltpu.roll(blk, 96, 1), pltpu.roll(blk, 32, 1))
            blk = blk * c64_ref[...] + partner * s64_ref[...]
        elif kind == "sig":
            blk = 1.0 / (1.0 + jnp.exp(-blk))
        if f_ref is not None:
            f_ref[:, c * LANES:(c + 1) * LANES] = blk
        if b_ref is not None:
            sb = blk if scales[c] == 1.0 else blk * scales[c]
            b_ref[:, c * LANES:(c + 1) * LANES] = sb.astype(BF16)


def _proj(x_bf, w_bf, tabs, kinds, scales=None, want_f32=True, want_bf16=True, tm=512):
    m, d = x_bf.shape
    n = w_bf.shape[1]
    tm = min(tm, m)
    scales = tuple(scales) if scales is not None else (1.0,) * (n // LANES)
    outs, specs = [], []
    if want_f32:
        outs.append(jax.ShapeDtypeStruct((m, n), F32))
        specs.append(pl.BlockSpec((tm, n), lambda i: (i, 0)))
    if want_bf16:
        outs.append(jax.ShapeDtypeStruct((m, n), BF16))
        specs.append(pl.BlockSpec((tm, n), lambda i: (i, 0)))
    tab = pl.BlockSpec((tm, LANES), lambda i: (i, 0))
    res = pl.pallas_call(
        functools.partial(_proj_kernel, kinds=tuple(kinds), scales=scales, want_f32=want_f32, want_bf16=want_bf16),
        grid=(m // tm,),
        in_specs=[pl.BlockSpec((tm, d), lambda i: (i, 0)), pl.BlockSpec((d, n), lambda i: (0, 0)), tab, tab, tab, tab],
        out_specs=specs, out_shape=outs, compiler_params=_cp("parallel"), name="proj",
    )(x_bf, w_bf, *tabs)
    return res


def _diff_finalize(a1, l1, a2, l2, lam_ref, g_ref, lam_init):
    lv = lam_ref[...]
    la = jnp.sum(lv[0:1] * lv[1:2], axis=1, keepdims=True)
    lb = jnp.sum(lv[2:3] * lv[3:4], axis=1, keepdims=True)
    lam = jnp.exp(la) - jnp.exp(lb) + lam_init
    o = a1 / l1 - lam * (a2 / l2)
    o = o * lax.rsqrt(jnp.mean(o * o, axis=-1, keepdims=True) + LN_EPS) * g_ref[...] * (1.0 - lam_init)
    return o


def _diff_attn_kernel(q_ref, k_ref, v_ref, lam_ref, g_ref, o_ref, *, t, lam_init):
    qi = pl.program_id(1)
    q = q_ref[...]
    lane = lax.broadcasted_iota(I32, q.shape, 1)
    zero = jnp.zeros_like(q)
    q1 = jnp.where(lane < A_HALF, q, zero)
    q2 = jnp.where(lane < A_HALF, zero, q)
    rowi = lax.broadcasted_iota(I32, (t, t), 0)
    coli = lax.broadcasted_iota(I32, (t, t), 1)

    def update(s, m, l, a, v):
        mn = jnp.maximum(m, jnp.max(s, axis=-1, keepdims=True))
        al = jnp.exp(m - mn)
        p = jnp.exp(s - mn)
        l = al * l + jnp.sum(p, axis=-1, keepdims=True)
        a = al * a + _dot(p.astype(BF16), v)
        return mn, l, a

    def tile(ti, carry, masked):
        m1, l1, a1, m2, l2, a2 = carry
        off = pl.multiple_of(ti * t, t)
        k = k_ref[pl.ds(off, t), :]
        v = v_ref[pl.ds(off, t), :]
        s1 = _dot_nt(q1, k)
        s2 = _dot_nt(q2, k)
        if masked:
            s1 = jnp.where(coli <= rowi, s1, NEG)
            s2 = jnp.where(coli <= rowi, s2, NEG)
        m1, l1, a1 = update(s1, m1, l1, a1, v)
        m2, l2, a2 = update(s2, m2, l2, a2, v)
        return m1, l1, a1, m2, l2, a2

    mi = jnp.full((t, 1), NEG, F32)
    li = jnp.zeros((t, 1), F32)
    ai = jnp.zeros((t, HEAD_DIM), F32)
    carry = lax.fori_loop(0, qi, lambda ti, c: tile(ti, c, False), (mi, li, ai, mi, li, ai))
    m1, l1, a1, m2, l2, a2 = tile(qi, carry, True)
    o_ref[...] = _diff_finalize(a1, l1, a2, l2, lam_ref, g_ref, lam_init).astype(o_ref.dtype)


def _diff_attn_prompt(q_bf, k_bf, v_bf, lam_vec, norm_g, lam_init, t=512):
    s = q_bf.shape[0]
    t = min(t, s)
    return pl.pallas_call(
        functools.partial(_diff_attn_kernel, t=t, lam_init=lam_init),
        grid=(A_HEADS, s // t),
        in_specs=[pl.BlockSpec((t, HEAD_DIM), lambda h, i: (i, h)),
                  pl.BlockSpec((s, HEAD_DIM), lambda h, i: (0, h)),
                  pl.BlockSpec((s, HEAD_DIM), lambda h, i: (0, h)),
                  pl.BlockSpec((4, A_HALF), lambda h, i: (0, 0)),
                  pl.BlockSpec((1, HEAD_DIM), lambda h, i: (0, 0))],
        out_specs=pl.BlockSpec((t, HEAD_DIM), lambda h, i: (i, h)),
        out_shape=jax.ShapeDtypeStruct((s, A_WIDTH), BF16),
        compiler_params=_cp("parallel", "parallel"), name="diff_attn",
    )(q_bf, k_bf, v_bf, lam_vec, norm_g.reshape(1, HEAD_DIM))


def _compress_kernel(kc_ref, vc_ref, cos_ref, sin_ref, kcb_ref, vcb_ref, kch, vch, *, nc, steps):
    i = pl.program_id(0)
    rows = kc_ref.shape[0]
    cpb = rows // CMP_STRIDE
    off = pl.multiple_of(i * cpb, cpb)
    kch[pl.ds(off, cpb), :] = jnp.sum(kc_ref[...].reshape(cpb, CMP_STRIDE, KV_WIDTH), axis=1)
    vch[pl.ds(off, cpb), :] = jnp.sum(vc_ref[...].reshape(cpb, CMP_STRIDE, KV_WIDTH), axis=1)

    @pl.when(i == steps - 1)
    def _():
        last = lax.broadcasted_iota(I32, (nc, KV_WIDTH), 0) == nc - 1

        def blocks(ch_ref):
            ch = ch_ref[...]
            nxt = pltpu.roll(ch, nc - 1, 0)
            return jnp.where(last, 0.0, (ch + nxt) / CMP_LEN)

        kb = blocks(kch)
        for c in range(B_KV):
            blk = kb[:, c * LANES:(c + 1) * LANES]
            blk = blk * cos_ref[...] + pltpu.roll(blk, 64, 1) * sin_ref[...]
            kcb_ref[:, c * LANES:(c + 1) * LANES] = blk.astype(BF16)
        vcb_ref[...] = blocks(vch).astype(BF16)


def _compress_prompt(k_f32, v_f32, cos_c, sin_c):
    s = k_f32.shape[0]
    nc = s // CMP_STRIDE
    steps = 4 if s >= 2048 else 1
    rows = s // steps
    return pl.pallas_call(
        functools.partial(_compress_kernel, nc=nc, steps=steps),
        grid=(steps,),
        in_specs=[pl.BlockSpec((rows, KV_WIDTH), lambda i: (i, 2)),
                  pl.BlockSpec((rows, KV_WIDTH), lambda i: (i, 2)),
                  pl.BlockSpec((nc, LANES), lambda i: (0, 0)),
                  pl.BlockSpec((nc, LANES), lambda i: (0, 0))],
        out_specs=[pl.BlockSpec((nc, KV_WIDTH), lambda i: (0, 0)), pl.BlockSpec((nc, KV_WIDTH), lambda i: (0, 0))],
        out_shape=[jax.ShapeDtypeStruct((nc, KV_WIDTH), BF16), jax.ShapeDtypeStruct((nc, KV_WIDTH), BF16)],
        scratch_shapes=[pltpu.VMEM((nc, KV_WIDTH), F32), pltpu.VMEM((nc, KV_WIDTH), F32)],
        compiler_params=_cp("arbitrary"), name="nsa_compress",
    )(k_f32, v_f32, cos_c, sin_c)


NSA_TQ = 128
NSA_TK = 512


def _softmax_rows(s):
    mx = jnp.max(s, axis=-1, keepdims=True)
    p = jnp.exp(s - mx)
    return p / jnp.sum(p, axis=-1, keepdims=True)


def _nsa_kernel(q_ref, kcb_ref, vcb_ref, ks_ref, vs_ref, kw_ref, vw_ref, g_ref, m_ref, e_ref, o_ref, v_scr, *, nc, nsp):
    b = pl.program_id(1)
    tq, g4 = NSA_TQ, B_GROUP
    q = q_ref[...]
    qs = jnp.concatenate([q[:, g * LANES:(g + 1) * LANES] for g in range(g4)], axis=0)
    pos = b * tq + lax.broadcasted_iota(I32, (tq, 1), 0)

    sc = _dot_nt(qs, kcb_ref[...]).reshape(g4, tq, nc)
    cend = CMP_STRIDE * lax.broadcasted_iota(I32, (1, nc), 1) + (CMP_LEN - 1)
    mc = cend <= pos
    pc = _softmax_rows(jnp.where(mc[None], sc, NEG))
    pc = pc * jnp.where(pos >= CMP_LEN - 1, 1.0, 0.0)[None]
    oc = _dot(pc.reshape(g4 * tq, nc).astype(BF16), vcb_ref[...])

    pcs = pc[0] + pc[1] + pc[2] + pc[3]
    hi, mid, lo = _split3(pcs)
    mm = m_ref[...]
    imp = _dot(hi, mm) + _dot(mid, mm) + _dot(lo, mm)
    imp_t = imp.T
    n_io = lax.broadcasted_iota(I32, (nsp, tq), 0)
    cur_t = lax.shift_right_logical(b * tq + lax.broadcasted_iota(I32, (nsp, tq), 1), 6)
    val = jnp.where(n_io > cur_t, NEG, imp_t)
    forced = jnp.where(n_io == 0, 1, jnp.where(n_io == cur_t, 1, jnp.where(n_io == cur_t - 1, 1, 0)))
    val = jnp.where(forced == 1, BIG, val)
    v_scr[...] = val

    def rank_body(n2, cnt):
        r = v_scr[pl.ds(n2, 1), :]
        beats = jnp.where(r > val, 1, jnp.where(r == val, jnp.where(n2 < n_io, 1, 0), 0))
        return cnt + beats

    cnt = lax.fori_loop(0, 2 * b + 2, rank_body, jnp.zeros((nsp, tq), I32))
    sel_t = jnp.where(cnt < SEL_TOP, jnp.where(n_io <= cur_t, 1.0, 0.0), 0.0)
    sel = sel_t.T.astype(BF16)

    def sel_body(t, carry):
        m, l, acc = carry
        off = pl.multiple_of(t * NSA_TK, NSA_TK)
        kt = ks_ref[pl.ds(off, NSA_TK), :]
        vt = vs_ref[pl.ds(off, NSA_TK), :]
        s = _dot_nt(qs, kt).reshape(g4, tq, NSA_TK)
        allowed = _dot(sel, e_ref[t])
        kpos = off + lax.broadcasted_iota(I32, (tq, NSA_TK), 1)
        ok = jnp.where(kpos <= pos, allowed, 0.0) > 0.5
        s = jnp.where(ok[None], s, NEG)
        mn = jnp.maximum(m, jnp.max(s, axis=-1, keepdims=True))
        al = jnp.exp(m - mn)
        p = jnp.exp(s - mn)
        l = al * l + jnp.sum(p, axis=-1, keepdims=True)
        pv = _dot(p.reshape(g4 * tq, NSA_TK).astype(BF16), vt).reshape(g4, tq, HEAD_DIM)
        return mn, l, al * acc + pv

    ntiles = (b * tq + tq + NSA_TK - 1) // NSA_TK
    m0 = jnp.full((g4, tq, 1), NEG, F32)
    l0 = jnp.zeros((g4, tq, 1), F32)
    a0 = jnp.zeros((g4, tq, HEAD_DIM), F32)
    _, ls, accs = lax.fori_loop(0, ntiles, sel_body, (m0, l0, a0))
    osel = accs / ls

    wlen = WINDOW + tq
    wstart = pl.multiple_of(jnp.maximum(b * tq - WINDOW, 0), tq)
    kwt = kw_ref[pl.ds(wstart, wlen), :]
    vwt = vw_ref[pl.ds(wstart, wlen), :]
    sw = _dot_nt(qs, kwt).reshape(g4, tq, wlen)
    kp = wstart + lax.broadcasted_iota(I32, (tq, wlen), 1)
    okw = jnp.where(kp <= pos, jnp.where(kp > pos - WINDOW, 1, 0), 0) == 1
    pw = _softmax_rows(jnp.where(okw[None], sw, NEG))
    ow = _dot(pw.reshape(g4 * tq, wlen).astype(BF16), vwt).reshape(g4, tq, HEAD_DIM)

    oc = oc.reshape(g4, tq, HEAD_DIM)
    gt = g_ref[...]
    for g in range(g4):
        og = gt[:, 3 * g:3 * g + 1] * oc[g] + gt[:, 3 * g + 1:3 * g + 2] * osel[g] + gt[:, 3 * g + 2:3 * g + 3] * ow[g]
        o_ref[:, g * LANES:(g + 1) * LANES] = og.astype(o_ref.dtype)


def _nsa_tables(s):
    nc = s // CMP_STRIDE
    ns = s // SEL_LEN
    nsp = -(-ns // LANES) * LANES
    j = jnp.arange(nc)[:, None]
    n = jnp.arange(nsp)[None, :]
    pool = ((j >= 4 * n - 1) & (j <= 4 * n + 3) & (n < ns) & (j < nc - 1)).astype(BF16)
    t = jnp.arange(s)[None, :]
    expand = ((t // SEL_LEN) == jnp.arange(nsp)[:, None]).astype(BF16)
    expand = expand.reshape(nsp, s // NSA_TK, NSA_TK).transpose(1, 0, 2)
    return pool, expand


def _nsa_prompt(q_bf, kcb, vcb, k_bf, v_bf, w_bf, gates, pool, expand):
    s = q_bf.shape[0]
    nc = s // CMP_STRIDE
    nsp = pool.shape[1]
    nt = s // NSA_TK
    full = lambda col: pl.BlockSpec((s, HEAD_DIM), col)
    return pl.pallas_call(
        functools.partial(_nsa_kernel, nc=nc, nsp=nsp),
        grid=(B_KV, s // NSA_TQ),
        in_specs=[pl.BlockSpec((NSA_TQ, B_GROUP * HEAD_DIM), lambda k, b: (b, 1 + k)),
                  pl.BlockSpec((nc, HEAD_DIM), lambda k, b: (0, k)),
                  pl.BlockSpec((nc, HEAD_DIM), lambda k, b: (0, k)),
                  full(lambda k, b: (0, 6 + k)), full(lambda k, b: (0, 6 + k)),
                  full(lambda k, b: (0, k)), full(lambda k, b: (0, 2 + k)),
                  pl.BlockSpec((NSA_TQ, LANES), lambda k, b: (b, k)),
                  pl.BlockSpec((nc, nsp), lambda k, b: (0, 0)),
                  pl.BlockSpec((nt, nsp, NSA_TK), lambda k, b: (0, 0, 0))],
        out_specs=pl.BlockSpec((NSA_TQ, B_GROUP * HEAD_DIM), lambda k, b: (b, k)),
        out_shape=jax.ShapeDtypeStruct((s, B_WIDTH), BF16),
        scratch_shapes=[pltpu.VMEM((nsp, NSA_TQ), F32)],
        compiler_params=_cp("parallel", "arbitrary"), name="nsa",
    )(q_bf, kcb, vcb, k_bf, v_bf, w_bf, w_bf, gates, pool, expand)


def _pool_kernel(u_ref, up_ref, pw_ref, ps_ref, o_ref, ext, *, tm):
    i = pl.program_id(0)
    cur = u_ref[...]
    prev = jnp.where(i > 0, up_ref[...], 0.0)
    ext[0:16, :] = prev
    ext[16:16 + tm, :] = cur
    pos = i * tm + lax.broadcasted_iota(I32, (tm, 1), 0)
    for g, w in enumerate(C_WINDOWS):
        cols = slice(g * C_GW, (g + 1) * C_GW)
        acc = cur[:, cols]
        for r in range(1, w):
            acc = acc + ext[16 - r:16 - r + tm, cols]
        cnt = jnp.minimum(pos + 1, w).astype(F32)
        d = acc / cnt - cur[:, cols]
        o = _dot(d.astype(BF16), pw_ref[g]) * ps_ref[:, cols]
        o_ref[:, cols] = o.astype(o_ref.dtype)


def _pool_prompt(u, pool_w_bf, pool_scale, tm=512):
    s = u.shape[0]
    tm = min(tm, s)
    return pl.pallas_call(
        functools.partial(_pool_kernel, tm=tm),
        grid=(s // tm,),
        in_specs=[pl.BlockSpec((tm, C_WIDTH), lambda i: (i, 0)),
                  pl.BlockSpec((16, C_WIDTH), lambda i: (jnp.maximum(i * (tm // 16) - 1, 0), 0)),
                  pl.BlockSpec((4, C_GW, C_GW), lambda i: (0, 0, 0)),
                  pl.BlockSpec((1, C_WIDTH), lambda i: (0, 0))],
        out_specs=pl.BlockSpec((tm, C_WIDTH), lambda i: (i, 0)),
        out_shape=jax.ShapeDtypeStruct((s, C_WIDTH), BF16),
        scratch_shapes=[pltpu.VMEM((tm + 16, C_WIDTH), F32)],
        compiler_params=_cp("parallel"), name="pool_mixer",
    )(u, u, pool_w_bf, pool_scale.reshape(1, C_WIDTH))


def _ln(y, g, b):
    mu = jnp.mean(y, axis=-1, keepdims=True)
    yc = y - mu
    var = jnp.mean(yc * yc, axis=-1, keepdims=True)
    return yc * lax.rsqrt(var + LN_EPS) * g + b


def _outproj_kernel(oa_ref, ob_ref, oc_ref, x_ref, w_ref, g_ref, b_ref, of_ref, ob16_ref, *, alpha):
    h = _dot(oa_ref[...], w_ref[0:A_WIDTH, :])
    h = h + _dot(ob_ref[...], w_ref[A_WIDTH:A_WIDTH + B_WIDTH, :])
    h = h + _dot(oc_ref[...], w_ref[A_WIDTH + B_WIDTH:, :])
    y = _ln(alpha * x_ref[...] + h, g_ref[...], b_ref[...])
    of_ref[...] = y
    ob16_ref[...] = y.astype(BF16)


def _outproj_ln(oa, ob, oc, x, w_bf, g, b, alpha, tm=256):
    m, d = x.shape
    tm = min(tm, m)
    row = lambda n: pl.BlockSpec((tm, n), lambda i: (i, 0))
    vec = pl.BlockSpec((1, d), lambda i: (0, 0))
    return pl.pallas_call(
        functools.partial(_outproj_kernel, alpha=alpha),
        grid=(m // tm,),
        in_specs=[row(A_WIDTH), row(B_WIDTH), row(C_WIDTH), row(d), pl.BlockSpec((d, d), lambda i: (0, 0)), vec, vec],
        out_specs=[row(d), row(d)],
        out_shape=[jax.ShapeDtypeStruct((m, d), F32), jax.ShapeDtypeStruct((m, d), BF16)],
        compiler_params=_cp("parallel"), name="outproj_ln",
    )(oa, ob, oc, x, w_bf, g.reshape(1, d), b.reshape(1, d))


def _ffn_kernel(xb_ref, x_ref, gate_ref, wg_ref, wu_ref, wd_ref, g_ref, b_ref, of_ref, ob16_ref, acc, *, alpha, gated):
    e, f = pl.program_id(1), pl.program_id(2)

    @pl.when((e == 0) & (f == 0))
    def _():
        acc[...] = jnp.zeros_like(acc)

    xb = xb_ref[...]
    hg = _dot(xb, wg_ref[...])
    hu = _dot(xb, wu_ref[...])
    h = hg * (1.0 / (1.0 + jnp.exp(-hg))) * hu
    y = _dot(h.astype(BF16), wd_ref[...])
    if gated:
        y = gate_ref[...] * y
    acc[...] += y

    @pl.when((e == pl.num_programs(1) - 1) & (f == pl.num_programs(2) - 1))
    def _():
        o = _ln(alpha * x_ref[...] + acc[...], g_ref[...], b_ref[...])
        of_ref[...] = o
        ob16_ref[...] = o.astype(BF16)


def _ffn_ln(xb, x, gate3, wg, wu, wd, g, b, alpha, gated, tm=512, tf=512):
    m, d = x.shape
    ne, _, ff = wg.shape
    tm = min(tm, m)
    vec = pl.BlockSpec((1, d), lambda i, e, f: (0, 0))
    return pl.pallas_call(
        functools.partial(_ffn_kernel, alpha=alpha, gated=gated),
        grid=(m // tm, ne, ff // tf),
        in_specs=[pl.BlockSpec((tm, d), lambda i, e, f: (i, 0)),
                  pl.BlockSpec((tm, d), lambda i, e, f: (i, 0)),
                  pl.BlockSpec((None, tm, 1), lambda i, e, f: (e, i, 0)),
                  pl.BlockSpec((None, d, tf), lambda i, e, f: (e, 0, f)),
                  pl.BlockSpec((None, d, tf), lambda i, e, f: (e, 0, f)),
                  pl.BlockSpec((None, tf, d), lambda i, e, f: (e, f, 0)),
                  vec, vec],
        out_specs=[pl.BlockSpec((tm, d), lambda i, e, f: (i, 0)), pl.BlockSpec((tm, d), lambda i, e, f: (i, 0))],
        out_shape=[jax.ShapeDtypeStruct((m, d), F32), jax.ShapeDtypeStruct((m, d), BF16)],
        scratch_shapes=[pltpu.VMEM((tm, d), F32)],
        compiler_params=_cp("parallel", "arbitrary", "arbitrary"), name="ffn_ln",
    )(xb, x, gate3, wg, wu, wd, g.reshape(1, d), b.reshape(1, d))


PROJ_SIZES = (A_WIDTH, A_WIDTH, A_WIDTH, B_WIDTH, KV_WIDTH, KV_WIDTH, KV_WIDTH, KV_WIDTH, KV_WIDTH, KV_WIDTH, 3 * B_HEADS, C_WIDTH)
K_KINDS = ("r64",) * 4 + ("none",) * 2 + ("r128",) * 2
V_KINDS = ("none",) * 8
W_KINDS = ("r128",) * 2 + ("none",) * 2
Q_KINDS = ("r64",) * 4 + ("r128",) * 8
Q_SCALES = (A_HALF ** -0.5,) * 4 + (HEAD_DIM ** -0.5,) * 8


def _split_w_in(w):
    parts, o = [], 0
    for n in PROJ_SIZES:
        parts.append(w[:, o:o + n])
        o += n
    qa, ka, va, qb, kc, vc, ks, vs, kw, vw, gb, u = parts
    d = w.shape[0]
    per_kv = 3 * B_GROUP
    zpad = jnp.zeros((d, LANES - per_kv), w.dtype)
    wg = jnp.concatenate([gb[:, :per_kv], zpad, gb[:, per_kv:], zpad], axis=1)
    cat = lambda *xs: jnp.concatenate(xs, axis=1).astype(BF16)
    return dict(k=cat(ka, kc, ks), v=cat(va, vc, vs), w=cat(kw, vw), q=cat(qa, qb), u=u.astype(BF16), g=wg.astype(BF16))


def _project_all(x_bf, w, tabs, tm=512):
    kf, kb = _proj(x_bf, w["k"], tabs, K_KINDS, tm=tm)
    vf, vb = _proj(x_bf, w["v"], tabs, V_KINDS, tm=tm)
    wf, wb = _proj(x_bf, w["w"], tabs, W_KINDS, tm=tm)
    (qb,) = _proj(x_bf, w["q"], tabs, Q_KINDS, scales=Q_SCALES, want_f32=False, tm=tm)
    (uf,) = _proj(x_bf, w["u"], tabs, ("none",) * 4, want_bf16=False, tm=tm)
    (gf,) = _proj(x_bf, w["g"], tabs, ("sig",) * 2, want_bf16=False, tm=tm)
    return kf, kb, vf, vb, wf, wb, qb, uf, gf


def _prompt_mixer(x, x_bf, w, w_out_bf, lam_vec, lam_init, norm_g, pool_w_bf, pool_scale, ln_g, ln_b, alpha, consts):
    s = x.shape[0]
    kf, kb, vf, vb, wf, wb, qb, uf, gf = _project_all(x_bf, w, consts["tabs"])
    oa = _diff_attn_prompt(qb, kb, vb, lam_vec, norm_g, lam_init)
    kcb, vcb = _compress_prompt(kf, vf, consts["cos_c"], consts["sin_c"])
    ob = _nsa_prompt(qb, kcb, vcb, kb, vb, wb, gf, consts["pool"], consts["expand"])
    oc = _pool_prompt(uf, pool_w_bf, pool_scale)
    xn, xn_bf = _outproj_ln(oa, ob, oc, x, w_out_bf, ln_g, ln_b, alpha)
    wbuf = min(WINDOW, s)
    return xn, xn_bf, kf, vf, wf[s - wbuf:], uf[s - POOL_HIST:]


def _prompt_consts(s):
    pos = jnp.arange(s, dtype=I32)
    tabs = _rope_tables(pos)
    nc = s // CMP_STRIDE
    cend = jnp.arange(nc, dtype=I32) * CMP_STRIDE + (CMP_LEN - 1)
    cos_c, sin_c, _, _ = _rope_tables(cend)
    pool, expand = _nsa_tables(s)
    return dict(tabs=tabs, cos_c=cos_c, sin_c=sin_c, pool=pool, expand=expand)


def _router_kernel(x_ref, rw_ref, gate_ref, idx_ref, gv_ref):
    xh, xm, xl = _split3(x_ref[...])
    wh, wm, wl = _split3(rw_ref[...])
    logits = _dot(xh, wh) + (_dot(xh, wm) + _dot(xm, wh)) + (_dot(xh, wl) + _dot(xm, wm) + _dot(xl, wh))
    lane = lax.broadcasted_iota(I32, logits.shape, 1).astype(F32)
    ninf = -jnp.inf
    lg = jnp.where(lane < N_EXPERTS, logits, ninf)
    m1 = jnp.max(lg, axis=-1, keepdims=True)
    i1 = jnp.min(jnp.where(lg == m1, lane, float(LANES)), axis=-1, keepdims=True)
    lg2 = jnp.where(lane == i1, ninf, lg)
    m2 = jnp.max(lg2, axis=-1, keepdims=True)
    i2 = jnp.min(jnp.where(lg2 == m2, lane, float(LANES)), axis=-1, keepdims=True)
    e2 = jnp.exp(m2 - m1)
    den = 1.0 + e2
    g1 = 1.0 / den
    g2 = e2 / den
    gate_ref[...] = jnp.where(lane == i1, g1, 0.0) + jnp.where(lane == i2, g2, 0.0)
    idx_ref[...] = jnp.where(lane == 0.0, i1, jnp.where(lane == 1.0, i2, 0.0)).astype(I32)
    gv_ref[...] = jnp.where(lane == 0.0, g1, jnp.where(lane == 1.0, g2, 0.0))


def _router(x, router_w, tm=512):
    m, d = x.shape
    tm = min(tm, m)
    rw = jnp.concatenate([router_w, jnp.zeros((d, LANES - N_EXPERTS), F32)], axis=1)
    blk = pl.BlockSpec((tm, LANES), lambda i: (i, 0))
    return pl.pallas_call(
        _router_kernel, grid=(m // tm,),
        in_specs=[pl.BlockSpec((tm, d), lambda i: (i, 0)), pl.BlockSpec((d, LANES), lambda i: (0, 0))],
        out_specs=[blk, blk, blk],
        out_shape=[jax.ShapeDtypeStruct((m, LANES), F32), jax.ShapeDtypeStruct((m, LANES), I32), jax.ShapeDtypeStruct((m, LANES), F32)],
        compiler_params=_cp("parallel"), name="router",
    )(x, rw)


MOE_TM = 512


def _route_tables(ids, gvals, tm):
    s = ids.shape[0]
    na = 2 * s
    flat_e = ids.reshape(-1)
    onehot = (flat_e[:, None] == jnp.arange(N_EXPERTS, dtype=I32)[None, :]).astype(I32)
    csum = jnp.cumsum(onehot, axis=0)
    rank = jnp.take_along_axis(csum, flat_e[:, None], axis=1)[:, 0] - 1
    counts = csum[-1]
    pcount = ((counts + tm - 1) // tm) * tm
    ends = jnp.cumsum(pcount)
    offs = ends - pcount
    dest = (offs[flat_e] + rank).astype(I32)
    nrows = na + N_EXPERTS * tm
    ntiles = nrows // tm
    src = jnp.zeros((nrows,), I32).at[dest].set(jnp.arange(na, dtype=I32) // 2)
    gsort = jnp.zeros((nrows,), F32).at[dest].set(gvals.reshape(-1))
    tstart = jnp.arange(ntiles, dtype=I32) * tm
    te = jnp.sum((tstart[:, None] >= ends[None, :]).astype(I32), axis=1)
    valid = (te < N_EXPERTS).astype(I32)
    last_e = jnp.max(jnp.where(pcount > 0, jnp.arange(N_EXPERTS, dtype=I32), 0))
    te = jnp.where(valid == 1, te, last_e).astype(I32)
    return src, gsort.reshape(nrows, 1), te, valid, dest


def _row_copy(src_hbm, row, dst, r, sem):
    return pltpu.make_async_copy(src_hbm.at[pl.ds(row, 1), :], dst.at[pl.ds(r, 1), :], sem)


def _moe_kernel(src_ref, te_ref, valid_ref, x_hbm, gs_ref, wg_ref, wu_ref, wd_ref, y_ref, xbuf, xb, acc, sem, *, tm):
    i, f = pl.program_id(0), pl.program_id(1)

    @pl.when(f == 0)
    def _():
        base = i * tm

        def issue(r, c):
            _row_copy(x_hbm, src_ref[base + r], xbuf, r, sem).start()
            return c

        def wait(r, c):
            _row_copy(x_hbm, 0, xbuf, r, sem).wait()
            return c

        lax.fori_loop(0, tm, issue, 0)
        lax.fori_loop(0, tm, wait, 0)
        xb[...] = xbuf[...].astype(BF16)
        acc[...] = jnp.zeros_like(acc)

    @pl.when(valid_ref[i] == 1)
    def _():
        x = xb[...]
        hg = _dot(x, wg_ref[...])
        hu = _dot(x, wu_ref[...])
        h = hg * (1.0 / (1.0 + jnp.exp(-hg))) * hu
        acc[...] += _dot(h.astype(BF16), wd_ref[...])

    @pl.when(f == pl.num_programs(1) - 1)
    def _():
        y_ref[...] = gs_ref[...] * acc[...]


def _moe_grouped(x, src, gsort, te, valid, wg, wu, wd, tm, tf=512):
    d = x.shape[1]
    ff = wg.shape[2]
    nrows = gsort.shape[0]
    grid_spec = pltpu.PrefetchScalarGridSpec(
        num_scalar_prefetch=3, grid=(nrows // tm, ff // tf),
        in_specs=[pl.BlockSpec(memory_space=pl.ANY),
                  pl.BlockSpec((tm, 1), lambda i, f, s_, e_, v_: (i, 0)),
                  pl.BlockSpec((None, d, tf), lambda i, f, s_, e_, v_: (e_[i], 0, f)),
                  pl.BlockSpec((None, d, tf), lambda i, f, s_, e_, v_: (e_[i], 0, f)),
                  pl.BlockSpec((None, tf, d), lambda i, f, s_, e_, v_: (e_[i], f, 0))],
        out_specs=pl.BlockSpec((tm, d), lambda i, f, s_, e_, v_: (i, 0)),
        scratch_shapes=[pltpu.VMEM((tm, d), F32), pltpu.VMEM((tm, d), BF16), pltpu.VMEM((tm, d), F32),
                        pltpu.SemaphoreType.DMA(())])
    return pl.pallas_call(
        functools.partial(_moe_kernel, tm=tm), grid_spec=grid_spec,
        out_shape=jax.ShapeDtypeStruct((nrows, d), F32),
        compiler_params=_cp("arbitrary", "arbitrary"), name="moe_grouped",
    )(src, te, valid, x, gsort, wg, wu, wd)


def _combine_kernel(dest_ref, y_hbm, x_ref, g_ref, b_ref, of_ref, ob16_ref, buf, sem, *, tm, alpha):
    base = 2 * pl.program_id(0) * tm

    def issue(t, c):
        _row_copy(y_hbm, dest_ref[base + 2 * t], buf.at[0], t, sem).start()
        _row_copy(y_hbm, dest_ref[base + 2 * t + 1], buf.at[1], t, sem).start()
        return c

    def wait(t, c):
        _row_copy(y_hbm, 0, buf.at[0], t, sem).wait()
        _row_copy(y_hbm, 0, buf.at[1], t, sem).wait()
        return c

    lax.fori_loop(0, tm, issue, 0)
    lax.fori_loop(0, tm, wait, 0)
    o = _ln(alpha * x_ref[...] + (buf[0] + buf[1]), g_ref[...], b_ref[...])
    of_ref[...] = o
    ob16_ref[...] = o.astype(BF16)


def _moe_combine_ln(y_sorted, dest, x, g, b, alpha, tm=256):
    m, d = x.shape
    row = pl.BlockSpec((tm, d), lambda i, d_: (i, 0))
    vec = pl.BlockSpec((1, d), lambda i, d_: (0, 0))
    grid_spec = pltpu.PrefetchScalarGridSpec(
        num_scalar_prefetch=1, grid=(m // tm,),
        in_specs=[pl.BlockSpec(memory_space=pl.ANY), row, vec, vec],
        out_specs=[row, row],
        scratch_shapes=[pltpu.VMEM((2, tm, d), F32), pltpu.SemaphoreType.DMA(())])
    return pl.pallas_call(
        functools.partial(_combine_kernel, tm=tm, alpha=alpha), grid_spec=grid_spec,
        out_shape=[jax.ShapeDtypeStruct((m, d), F32), jax.ShapeDtypeStruct((m, d), BF16)],
        compiler_params=_cp("arbitrary"), name="moe_combine_ln",
    )(dest, y_sorted, x, g.reshape(1, d), b.reshape(1, d))


def _moe_prompt(xn, router_w, wg, wu, wd, g, b, alpha):
    _, idx, gv = _router(xn, router_w)
    src, gsort, te, valid, dest = _route_tables(idx[:, :TOP_K], gv[:, :TOP_K], MOE_TM)
    y_sorted = _moe_grouped(xn, src, gsort, te, valid, wg, wu, wd, MOE_TM)
    return _moe_combine_ln(y_sorted, dest, xn, g, b, alpha)


def _head_rows(row, width=HEAD_DIM, reps=2, heads=A_HEADS, first=0):
    parts = []
    for h in range(heads):
        parts += [row[:, (first + h) * width:(first + h + 1) * width]] * reps
    return jnp.concatenate(parts, axis=0)


SDIFF_PAGES = 8


def _sdiff_kernel(pt_ref, q_ref, *refs, lam_init, npg):
    page_refs = refs[:2 * npg]
    kn_ref, vn_ref, lam_ref, g_ref, o_ref, kch_ref, vch_ref, qm, m_s, l_s, acc = refs[2 * npg:]
    p = pl.program_id(1)
    nq = 2 * A_HEADS
    rows = lax.broadcasted_iota(I32, (nq, HEAD_DIM), 0)
    lane = lax.broadcasted_iota(I32, (nq, HEAD_DIM), 1)

    @pl.when(p == 0)
    def _():
        q8 = _head_rows(q_ref[...].astype(F32))
        qm[...] = jnp.where((lane // A_HALF) == (rows % 2), q8, 0.0)
        m_s[...] = jnp.full_like(m_s, NEG)
        l_s[...] = jnp.zeros_like(l_s)
        acc[...] = jnp.zeros_like(acc)

    q8 = qm[...]
    q8b = q8.astype(BF16)
    nchunk = PAGE_SIZE // CMP_STRIDE
    flat = PAGE_SIZE * CACHE_HEADS
    srow = lax.broadcasted_iota(I32, (nq, flat), 0)
    scol = lax.broadcasted_iota(I32, (nq, flat), 1)
    own = (scol % CACHE_HEADS) == (srow // 2)
    s_parts, v_flat = [], []
    for j in range(npg):
        kp_ref, vp_ref = page_refs[j], page_refs[npg + j]
        kp = kp_ref[...]
        vp = vp_ref[...]
        s_parts.append(jnp.where(own, _dot_nt(q8b, kp.reshape(flat, HEAD_DIM).astype(BF16)), NEG))
        v_flat.append(vp.reshape(flat, HEAD_DIM).astype(BF16))
        kch_ref[j * nchunk:(j + 1) * nchunk] = jnp.sum(kp.reshape(nchunk, CMP_STRIDE, CACHE_HEADS, HEAD_DIM), axis=1)
        vch_ref[j * nchunk:(j + 1) * nchunk] = jnp.sum(vp.reshape(nchunk, CMP_STRIDE, CACHE_HEADS, HEAD_DIM), axis=1)
    s = jnp.concatenate(s_parts, axis=1)
    mn = jnp.maximum(m_s[...], jnp.max(s, axis=-1, keepdims=True))
    al = jnp.exp(m_s[...] - mn)
    pr = jnp.exp(s - mn)
    l_s[...] = al * l_s[...] + jnp.sum(pr, axis=-1, keepdims=True)
    prb = pr.astype(BF16)
    pv = _dot(prb[:, 0:flat], v_flat[0])
    for j in range(1, npg):
        pv = pv + _dot(prb[:, j * flat:(j + 1) * flat], v_flat[j])
    acc[...] = al * acc[...] + pv
    m_s[...] = mn

    @pl.when(p == pl.num_programs(1) - 1)
    def _():
        k8 = _head_rows(kn_ref[...])
        v8 = _head_rows(vn_ref[...])
        sn = jnp.sum(q8 * k8, axis=-1, keepdims=True)
        m2 = jnp.maximum(m_s[...], sn)
        a2 = jnp.exp(m_s[...] - m2)
        pn = jnp.exp(sn - m2)
        lt = a2 * l_s[...] + pn
        o8 = (a2 * acc[...] + pn * v8) / lt
        lv = lam_ref[...]
        la = jnp.sum(lv[0:1] * lv[1:2], axis=1, keepdims=True)
        lb = jnp.sum(lv[2:3] * lv[3:4], axis=1, keepdims=True)
        lam = jnp.exp(la) - jnp.exp(lb) + lam_init
        outs = []
        for h in range(A_HEADS):
            o = o8[2 * h:2 * h + 1] - lam * o8[2 * h + 1:2 * h + 2]
            o = o * lax.rsqrt(jnp.mean(o * o, axis=-1, keepdims=True) + LN_EPS) * g_ref[...] * (1.0 - lam_init)
            outs.append(o)
        o_ref[...] = jnp.concatenate(outs, axis=1).astype(o_ref.dtype)


def _sample_diff(layer, pt, q3, cache_k, cache_v, kn3, vn3, lam_vec, norm_g, lam_init):
    nb, pages = pt.shape
    npg = math.gcd(SDIFF_PAGES, pages)
    nchunk = PAGE_SIZE // CMP_STRIDE

    def page_spec(j):
        return pl.BlockSpec((None, None, PAGE_SIZE, CACHE_HEADS, HEAD_DIM),
                            lambda b, p, t: (layer, t[b, p * npg + j], 0, 0, 0))

    row = lambda n: pl.BlockSpec((None, 1, n), lambda b, p, t: (b, 0, 0))
    grid_spec = pltpu.PrefetchScalarGridSpec(
        num_scalar_prefetch=1, grid=(nb, pages // npg),
        in_specs=[row(A_WIDTH)] + [page_spec(j) for j in range(npg)] * 2 + [
                  row(A_WIDTH), row(A_WIDTH),
                  pl.BlockSpec((4, A_HALF), lambda b, p, t: (0, 0)),
                  pl.BlockSpec((1, HEAD_DIM), lambda b, p, t: (0, 0))],
        out_specs=[row(A_WIDTH),
                   pl.BlockSpec((None, npg * nchunk, CACHE_HEADS, HEAD_DIM), lambda b, p, t: (b, p, 0, 0)),
                   pl.BlockSpec((None, npg * nchunk, CACHE_HEADS, HEAD_DIM), lambda b, p, t: (b, p, 0, 0))],
        scratch_shapes=[pltpu.VMEM((2 * A_HEADS, HEAD_DIM), F32), pltpu.VMEM((2 * A_HEADS, 1), F32),
                        pltpu.VMEM((2 * A_HEADS, 1), F32), pltpu.VMEM((2 * A_HEADS, HEAD_DIM), F32)])
    chunk_sums = jax.ShapeDtypeStruct((nb, pages * nchunk, CACHE_HEADS, HEAD_DIM), F32)
    return pl.pallas_call(
        functools.partial(_sdiff_kernel, lam_init=lam_init, npg=npg), grid_spec=grid_spec,
        out_shape=[jax.ShapeDtypeStruct((nb, 1, A_WIDTH), BF16), chunk_sums, chunk_sums],
        compiler_params=_cp("parallel", "arbitrary"), name="sample_diff",
    )(pt, q3, *([cache_k] * npg), *([cache_v] * npg), kn3, vn3, lam_vec, norm_g.reshape(1, HEAD_DIM))


def _group_rows(q_row):
    q4 = jnp.concatenate([q_row[:, g * HEAD_DIM:(g + 1) * HEAD_DIM] for g in range(B_GROUP)], axis=0)
    return jnp.concatenate([q4, jnp.zeros_like(q4)], axis=0)


def _snsa_kernel(q0_ref, q1_ref, kch_ref, vch_ref, kn_ref, vn_ref, cos_ref, sin_ref, m_ref, st_ref, wn_ref, g_ref,
                 idx_ref, ocw_ref, nw_ref, *, pos, nc, nsp):
    rowc = lax.broadcasted_iota(I32, (nc, KV_WIDTH), 0)
    kn = kn_ref[...]
    vn = vn_ref[...]
    c0 = A_WIDTH

    def blocks(ch, new_row):
        nxt = jnp.where(rowc == nc - 1, new_row, pltpu.roll(ch, nc - 1, 0))
        return (ch + nxt) / CMP_LEN

    def cmp_heads(ref):
        return jnp.concatenate([ref[:, A_HEADS + c, :] for c in range(B_KV)], axis=1)

    kb = blocks(cmp_heads(kch_ref), kn[:, c0:c0 + KV_WIDTH])
    vcb = blocks(cmp_heads(vch_ref), vn[:, c0:c0 + KV_WIDTH]).astype(BF16)
    kcb = []
    for c in range(B_KV):
        blk = kb[:, c * LANES:(c + 1) * LANES]
        kcb.append((blk * cos_ref[...] + pltpu.roll(blk, 64, 1) * sin_ref[...]).astype(BF16))

    cend = CMP_STRIDE * lax.broadcasted_iota(I32, (8, nc), 1) + (CMP_LEN - 1)
    mc = cend <= pos
    any_valid = 1.0 if pos >= CMP_LEN - 1 else 0.0
    cur = pos // SEL_LEN
    st = st_ref[...]
    wb = st.shape[0]
    wn = wn_ref[...]
    gt = g_ref[...]
    wcol = lax.broadcasted_iota(I32, (8, wb), 1)
    okw = (pos - wb + wcol) > pos - WINDOW
    n_row = lax.broadcasted_iota(I32, (1, nsp), 1)
    ii = lax.broadcasted_iota(I32, (nsp, nsp), 0)
    jj = lax.broadcasted_iota(I32, (nsp, nsp), 1)
    rr = lax.broadcasted_iota(I32, (SEL_TOP, nsp), 0).astype(F32)
    nn = lax.broadcasted_iota(I32, (SEL_TOP, nsp), 1).astype(F32)
    outs = []
    for k, q_ref in enumerate((q0_ref, q1_ref)):
        qs = _group_rows(q_ref[...])
        sc = _dot_nt(qs, kcb[k])
        pc = _softmax_rows(jnp.where(mc, sc, NEG)) * any_valid
        oc = _dot(pc.astype(BF16), vcb[:, k * LANES:(k + 1) * LANES])
        pcs = pc[0:1] + pc[1:2] + pc[2:3] + pc[3:4]
        hi, mid, lo = _split3(jnp.broadcast_to(pcs, (8, nc)))
        mm = m_ref[...]
        imp = (_dot(hi, mm) + _dot(mid, mm) + _dot(lo, mm))[0:1]
        val = jnp.where(n_row > cur, NEG, imp)
        forced = jnp.where(n_row == 0, 1, jnp.where(n_row == cur, 1, jnp.where(n_row == cur - 1, 1, 0)))
        val = jnp.where(forced == 1, BIG, val)
        a_m = jnp.broadcast_to(val, (nsp, nsp))
        b_m = a_m.T
        beats = jnp.where(b_m > a_m, 1.0, jnp.where(b_m == a_m, jnp.where(ii < jj, 1.0, 0.0), 0.0))
        rank = jnp.sum(beats, axis=0, keepdims=True)
        hit = jnp.broadcast_to(rank, (SEL_TOP, nsp)) == rr
        picked = jnp.sum(jnp.where(hit, nn, 0.0), axis=1, keepdims=True)
        idx_ref[k] = jnp.broadcast_to(picked, (SEL_TOP, LANES)).astype(I32)

        kw_s = st[:, k * LANES:(k + 1) * LANES].astype(BF16)
        vw_s = st[:, (B_KV + k) * LANES:(B_KV + k + 1) * LANES].astype(BF16)
        sw = jnp.where(okw, _dot_nt(qs, kw_s), NEG)
        qf = qs.astype(F32)
        s_new = jnp.sum(qf * wn[:, k * LANES:(k + 1) * LANES], axis=-1, keepdims=True)
        mw = jnp.maximum(jnp.max(sw, axis=-1, keepdims=True), s_new)
        pw = jnp.exp(sw - mw)
        pn = jnp.exp(s_new - mw)
        ow = (_dot(pw.astype(BF16), vw_s) + pn * wn[:, (B_KV + k) * LANES:(B_KV + k + 1) * LANES]) / (jnp.sum(pw, axis=-1, keepdims=True) + pn)
        for g in range(B_GROUP):
            c = k * LANES + 3 * g
            outs.append(gt[:, c:c + 1] * oc[g:g + 1] + gt[:, c + 2:c + 3] * ow[g:g + 1])
    ocw_ref[...] = jnp.concatenate(outs, axis=1)
    roww = lax.broadcasted_iota(I32, st.shape, 0)
    nw_ref[...] = jnp.where(roww == wb - 1, wn, pltpu.roll(st, wb - 1, 0))


def _sample_nsa_a(q3, kch, vch, kn3, vn3, cos_c, sin_c, pool, st, wn3, g3, pos):
    nb = q3.shape[0]
    nc = kch.shape[1]
    nsp = pool.shape[1]
    wb = st.shape[1]
    b3 = lambda n, c=0: pl.BlockSpec((None, 1, n), lambda b: (b, 0, c))
    return pl.pallas_call(
        functools.partial(_snsa_kernel, pos=pos, nc=nc, nsp=nsp),
        grid=(nb,),
        in_specs=[b3(4 * HEAD_DIM, 1), b3(4 * HEAD_DIM, 2),
                  pl.BlockSpec((None, nc, CACHE_HEADS, HEAD_DIM), lambda b: (b, 0, 0, 0)),
                  pl.BlockSpec((None, nc, CACHE_HEADS, HEAD_DIM), lambda b: (b, 0, 0, 0)),
                  b3(8 * HEAD_DIM), b3(8 * HEAD_DIM),
                  pl.BlockSpec((nc, LANES), lambda b: (0, 0)), pl.BlockSpec((nc, LANES), lambda b: (0, 0)),
                  pl.BlockSpec((nc, nsp), lambda b: (0, 0)),
                  pl.BlockSpec((None, wb, 4 * HEAD_DIM), lambda b: (b, 0, 0)),
                  b3(4 * HEAD_DIM), b3(2 * LANES)],
        out_specs=[pl.BlockSpec((None, B_KV, SEL_TOP, LANES), lambda b: (b, 0, 0, 0)),
                   b3(B_WIDTH),
                   pl.BlockSpec((None, wb, 4 * HEAD_DIM), lambda b: (b, 0, 0))],
        out_shape=[jax.ShapeDtypeStruct((nb, B_KV, SEL_TOP, LANES), I32),
                   jax.ShapeDtypeStruct((nb, 1, B_WIDTH), F32),
                   jax.ShapeDtypeStruct((nb, wb, 4 * HEAD_DIM), F32)],
        compiler_params=_cp("parallel"), name="sample_nsa_a",
    )(q3, q3, kch, vch, kn3, vn3, cos_c, sin_c, pool, st, wn3, g3)


def _ssel_kernel(pt_ref, idx_ref, q_ref, *refs, nblk):
    blk_refs = refs[:2 * SEL_TOP]
    kn_ref, vn_ref, g_ref, ocw_ref, o_ref = refs[2 * SEL_TOP:]
    b, k = pl.program_id(0), pl.program_id(1)
    qs = _group_rows(q_ref[...])
    sel_head = A_HEADS + B_KV + k
    flat = SEL_LEN * CACHE_HEADS
    own = (lax.broadcasted_iota(I32, (8, flat), 1) % CACHE_HEADS) == sel_head

    s_new = jnp.sum(qs.astype(F32) * kn_ref[...], axis=-1, keepdims=True)
    s_parts = []
    for r in range(SEL_TOP):
        valid = idx_ref[(b * B_KV + k) * SEL_TOP + r] < nblk
        kb = blk_refs[r][...].reshape(flat, HEAD_DIM).astype(BF16)
        s_parts.append(jnp.where(own, jnp.where(valid, _dot_nt(qs, kb), NEG), NEG))
    s = jnp.concatenate(s_parts, axis=1)
    mx = jnp.maximum(jnp.max(s, axis=-1, keepdims=True), s_new)
    p = jnp.exp(s - mx)
    pn = jnp.exp(s_new - mx)
    den = jnp.sum(p, axis=-1, keepdims=True) + pn
    pb = p.astype(BF16)
    acc = pn * vn_ref[...]
    for r in range(SEL_TOP):
        vb = blk_refs[SEL_TOP + r][...].reshape(flat, HEAD_DIM).astype(BF16)
        acc = acc + _dot(pb[:, r * flat:(r + 1) * flat], vb)
    osel = acc / den
    gt = g_ref[...]
    ocw = ocw_ref[...]
    outs = []
    for g in range(B_GROUP):
        outs.append(ocw[:, g * LANES:(g + 1) * LANES] + gt[:, 3 * g + 1:3 * g + 2] * osel[g:g + 1])
    o_ref[...] = jnp.concatenate(outs, axis=1).astype(o_ref.dtype)


def _sample_sel(layer, pt, idx, q3, cache_k, cache_v, kn3, vn3, g3, ocw):
    nb = q3.shape[0]
    per = PAGE_SIZE // SEL_LEN
    nblk = pt.shape[1] * per

    def blk_spec(r):
        def blk_map(b, k, t, ix):
            n = jnp.minimum(ix[(b * B_KV + k) * SEL_TOP + r], nblk - 1)
            return (layer, t[b, n // per], n % per, 0, 0)
        return pl.BlockSpec((None, None, SEL_LEN, CACHE_HEADS, HEAD_DIM), blk_map)

    grid_spec = pltpu.PrefetchScalarGridSpec(
        num_scalar_prefetch=2, grid=(nb, B_KV),
        in_specs=[pl.BlockSpec((None, 1, 4 * HEAD_DIM), lambda b, k, t, ix: (b, 0, 1 + k))]
                 + [blk_spec(r) for r in range(SEL_TOP)] * 2 + [
                  pl.BlockSpec((None, 1, HEAD_DIM), lambda b, k, t, ix: (b, 0, 6 + k)),
                  pl.BlockSpec((None, 1, HEAD_DIM), lambda b, k, t, ix: (b, 0, 6 + k)),
                  pl.BlockSpec((None, 1, LANES), lambda b, k, t, ix: (b, 0, k)),
                  pl.BlockSpec((None, 1, 4 * HEAD_DIM), lambda b, k, t, ix: (b, 0, k))],
        out_specs=pl.BlockSpec((None, 1, 4 * HEAD_DIM), lambda b, k, t, ix: (b, 0, k)))
    return pl.pallas_call(
        functools.partial(_ssel_kernel, nblk=nblk), grid_spec=grid_spec,
        out_shape=jax.ShapeDtypeStruct((nb, 1, B_WIDTH), BF16),
        compiler_params=_cp("parallel", "parallel"), name="sample_sel",
    )(pt, idx, q3, *([cache_k] * SEL_TOP), *([cache_v] * SEL_TOP), kn3, vn3, g3, ocw)


def _spool_kernel(st_ref, u_ref, pw_ref, ps_ref, o_ref, ns_ref):
    u = u_ref[...]
    for g, w in enumerate(C_WINDOWS):
        cols = slice(g * C_GW, (g + 1) * C_GW)
        acc = u[:, cols]
        for r in range(1, w):
            o0 = (POOL_HIST - r) * C_WIDTH + g * C_GW
            acc = acc + st_ref[:, o0:o0 + C_GW]
        d = acc / float(w) - u[:, cols]
        o_ref[:, cols] = (_dot(d.astype(BF16), pw_ref[g]) * ps_ref[:, cols]).astype(o_ref.dtype)
    ns_ref[:, 0:(POOL_HIST - 1) * C_WIDTH] = st_ref[:, C_WIDTH:]
    ns_ref[:, (POOL_HIST - 1) * C_WIDTH:] = u


def _sample_pool(st2, u, pool_w_bf, pool_scale):
    nb = u.shape[0]
    return pl.pallas_call(
        _spool_kernel,
        out_shape=[jax.ShapeDtypeStruct((nb, C_WIDTH), BF16), jax.ShapeDtypeStruct(st2.shape, F32)],
        compiler_params=pltpu.CompilerParams(vmem_limit_bytes=VMEM_LIMIT), name="sample_pool",
    )(st2, u, pool_w_bf, pool_scale.reshape(1, C_WIDTH))


def _sample_mixer(x, x_bf, w, w_out_bf, lam_vec, lam_init, norm_g, pool_w_bf, pool_scale, ln_g, ln_b, alpha,
                  layer, cache_k, cache_v, st_win, st_pool, pt, sconsts):
    nb = x.shape[0]
    past = pt.shape[1] * PAGE_SIZE
    kf, _, vf, _, wf, _, qb, uf, gf = _project_all(x_bf, w, sconsts["tabs"], tm=nb)
    q3, kn3, vn3 = qb.reshape(nb, 1, -1), kf.reshape(nb, 1, -1), vf.reshape(nb, 1, -1)
    oa, kch, vch = _sample_diff(layer, pt, q3, cache_k, cache_v, kn3, vn3, lam_vec, norm_g, lam_init)
    wb = st_win.shape[1]
    idx4, ocw, new_win = _sample_nsa_a(q3, kch, vch, kn3, vn3, sconsts["cos_c"], sconsts["sin_c"], sconsts["pool"],
                                       st_win.reshape(nb, wb, 4 * HEAD_DIM), wf.reshape(nb, 1, -1), gf.reshape(nb, 1, -1), past)
    ob = _sample_sel(layer, pt, idx4[:, :, :, 0].reshape(-1), q3, cache_k, cache_v, kn3, vn3, gf.reshape(nb, 1, -1), ocw)
    oc, new_pool = _sample_pool(st_pool.reshape(nb, POOL_HIST * C_WIDTH), uf, pool_w_bf, pool_scale)
    xn, xn_bf = _outproj_ln(oa.reshape(nb, A_WIDTH), ob.reshape(nb, B_WIDTH), oc, x, w_out_bf, ln_g, ln_b, alpha)
    return xn, xn_bf, kf, vf, new_win, new_pool


def _sample_consts(nb, past):
    pos = jnp.full((nb,), past, I32)
    tabs = _rope_tables(pos)
    nc = past // CMP_STRIDE
    cend = jnp.arange(nc, dtype=I32) * CMP_STRIDE + (CMP_LEN - 1)
    cos_c, sin_c, _, _ = _rope_tables(cend)
    ns = past // SEL_LEN + 1
    nsp = -(-ns // LANES) * LANES
    j = jnp.arange(nc)[:, None]
    n = jnp.arange(nsp)[None, :]
    pool = ((j >= 4 * n - 1) & (j <= 4 * n + 3) & (n < ns)).astype(BF16)
    return dict(tabs=tabs, cos_c=cos_c, sin_c=sin_c, pool=pool)


def kernel(x_prompt, x_sample, cache_k, cache_v, state_win, state_pool, page_table, w_in, w_out, diff_lambda, diff_norm_g, pool_w, pool_scale, ln1_g, ln1_b, ln2_g, ln2_b, ffn_w_gate, ffn_w_up, ffn_w_down, moe_router, moe_w_gate, moe_w_up, moe_w_down):
    depth = w_in.shape[0]
    alpha = (2 * depth) ** 0.25
    s = x_prompt.shape[1]
    nb = x_sample.shape[0]
    past = page_table.shape[1] * PAGE_SIZE
    pconsts = _prompt_consts(s)
    sconsts = _sample_consts(nb, past)
    xp, xs = x_prompt[0], x_sample[:, 0]
    xp_bf, xs_bf = xp.astype(BF16), xs.astype(BF16)
    outs = [[] for _ in range(8)]
    for l in range(depth):
        lam_init = 0.8 - 0.6 * math.exp(-0.3 * l)
        w = _split_w_in(w_in[l])
        w_out_bf = w_out[l].astype(BF16)
        pw_bf = pool_w[l].astype(BF16)
        xp, xp_bf, kf, vf, wf, uf = _prompt_mixer(xp, xp_bf, w, w_out_bf, diff_lambda[l], lam_init, diff_norm_g[l],
                                                  pw_bf, pool_scale[l], ln1_g[l], ln1_b[l], alpha, pconsts)
        xs, xs_bf, skf, svf, swin, spool = _sample_mixer(xs, xs_bf, w, w_out_bf, diff_lambda[l], lam_init, diff_norm_g[l],
                                                         pw_bf, pool_scale[l], ln1_g[l], ln1_b[l], alpha,
                                                         l, cache_k, cache_v, state_win[l], state_pool[l], page_table, sconsts)
        for o, v in zip(outs, (kf, vf, wf, uf, skf, svf, swin, spool)):
            o.append(v)
        i = l // 2
        if l % 2 == 0:
            wg, wu, wd = (a[i:i + 1].astype(BF16) for a in (ffn_w_gate, ffn_w_up, ffn_w_down))
            xp, xp_bf = _ffn_ln(xp_bf, xp, jnp.ones((1, s, 1), F32), wg, wu, wd, ln2_g[l], ln2_b[l], alpha, gated=False)
            xs, xs_bf = _ffn_ln(xs_bf, xs, jnp.ones((1, nb, 1), F32), wg, wu, wd, ln2_g[l], ln2_b[l], alpha, gated=False)
        else:
            wg, wu, wd = (a[i].astype(BF16) for a in (moe_w_gate, moe_w_up, moe_w_down))
            xp, xp_bf = _moe_prompt(xp, moe_router[i], wg, wu, wd, ln2_g[l], ln2_b[l], alpha)
            gate_s, _, _ = _router(xs, moe_router[i])
            gate3 = gate_s[:, :N_EXPERTS].T.reshape(N_EXPERTS, nb, 1)
            xs, xs_bf = _ffn_ln(xs_bf, xs, gate3, wg, wu, wd, ln2_g[l], ln2_b[l], alpha, gated=True)
    pk, pv, pwn, ppl, sk, sv, sw, sp = outs
    wbuf = min(WINDOW, s)
    swb = state_win.shape[2]
    return (xp[None], xs[:, None],
            jnp.stack(pk).reshape(depth, 1, s, CACHE_HEADS, HEAD_DIM),
            jnp.stack(pv).reshape(depth, 1, s, CACHE_HEADS, HEAD_DIM),
            jnp.stack(pwn).reshape(depth, 1, wbuf, 2, B_KV, HEAD_DIM),
            jnp.stack(ppl).reshape(depth, 1, POOL_HIST, C_WIDTH),
            jnp.stack(sk).reshape(depth, nb, 1, CACHE_HEADS, HEAD_DIM),
            jnp.stack(sv).reshape(depth, nb, 1, CACHE_HEADS, HEAD_DIM),
            jnp.stack(sw).reshape(depth, nb, swb, 2, B_KV, HEAD_DIM),
            jnp.stack(sp).reshape(depth, nb, POOL_HIST, C_WIDTH))
```

```python
import functools
import math

import jax
import jax.numpy as jnp
from jax import lax
from jax.experimental import pallas as pl
from jax.experimental.pallas import tpu as pltpu

F32 = jnp.float32
BF16 = jnp.bfloat16
I32 = jnp.int32

HEAD_DIM = 128
A_HEADS = 4
A_HALF = 64
A_WIDTH = 512
B_HEADS = 8
B_KV = 2
B_GROUP = 4
B_WIDTH = 1024
KV_WIDTH = 256
CMP_STRIDE = 16
CMP_LEN = 32
SEL_LEN = 64
SEL_TOP = 16
WINDOW = 512
C_WIDTH = 512
C_WINDOWS = (2, 4, 8, 16)
C_GW = 128
POOL_HIST = 15
CACHE_HEADS = 8
PAGE_SIZE = 128
N_EXPERTS = 8
TOP_K = 2
ROPE_THETA = 10000.0
LN_EPS = 1e-5
NEG = -1e30
BIG = 1e30

LANES = 128
VMEM_LIMIT = 56 * 2 ** 20


def _cp(*sem):
    return pltpu.CompilerParams(dimension_semantics=sem, vmem_limit_bytes=VMEM_LIMIT)


def _dot(a, b):
    return jnp.dot(a, b, preferred_element_type=F32)


def _dot_nt(a, b):
    return lax.dot_general(a, b, (((1,), (1,)), ((), ())), preferred_element_type=F32)


def _split3(x):
    hi = x.astype(BF16)
    r1 = x - hi.astype(F32)
    mid = r1.astype(BF16)
    lo = (r1 - mid.astype(F32)).astype(BF16)
    return hi, mid, lo


def _rope_tables(pos):
    lane = jnp.arange(LANES)
    p = pos.astype(F32)[:, None]
    inv128 = ROPE_THETA ** (-(lane % 64).astype(F32) / 64)
    a128 = p * inv128[None, :]
    cos128 = jnp.cos(a128)
    sin128 = jnp.where(lane[None, :] < 64, -jnp.sin(a128), jnp.sin(a128))
    inv64 = ROPE_THETA ** (-(lane % 32).astype(F32) / 32)
    a64 = p * inv64[None, :]
    cos64 = jnp.cos(a64)
    sin64 = jnp.where((lane[None, :] % 64) < 32, -jnp.sin(a64), jnp.sin(a64))
    return cos128, sin128, cos64, sin64


def _proj_kernel(x_ref, w_ref, c128_ref, s128_ref, c64_ref, s64_ref, *out_refs, kinds, scales, want_f32, want_bf16):
    acc = _dot(x_ref[...], w_ref[...])
    tm = acc.shape[0]
    lane = lax.broadcasted_iota(I32, (tm, LANES), 1)
    first = (lane % 64) < 32
    refs = list(out_refs)
    f_ref = refs.pop(0) if want_f32 else None
    b_ref = refs.pop(0) if want_bf16 else None
    for c, kind in enumerate(kinds):
        blk = acc[:, c * LANES:(c + 1) * LANES]
        if kind == "r128":
            blk = blk * c128_ref[...] + pltpu.roll(blk, 64, 1) * s128_ref[...]
        elif kind == "r64":
            partner = jnp.where(first, pltpu.roll(blk, 96, 1), pltpu.roll(blk, 32, 1))
            blk = blk * c64_ref[...] + partner * s64_ref[...]
        elif kind == "sig":
            blk = 1.0 / (1.0 + jnp.exp(-blk))
        if f_ref is not None:
            f_ref[:, c * LANES:(c + 1) * LANES] = blk
        if b_ref is not None:
            sb = blk if scales[c] == 1.0 else blk * scales[c]
            b_ref[:, c * LANES:(c + 1) * LANES] = sb.astype(BF16)


def _proj(x_bf, w_bf, tabs, kinds, scales=None, want_f32=True, want_bf16=True, tm=512):
    m, d = x_bf.shape
    n = w_bf.shape[1]
    tm = min(tm, m)
    scales = tuple(scales) if scales is not None else (1.0,) * (n // LANES)
    outs, specs = [], []
    if want_f32:
        outs.append(jax.ShapeDtypeStruct((m, n), F32))
        specs.append(pl.BlockSpec((tm, n), lambda i: (i, 0)))
    if want_bf16:
        outs.append(jax.ShapeDtypeStruct((m, n), BF16))
        specs.append(pl.BlockSpec((tm, n), lambda i: (i, 0)))
    tab = pl.BlockSpec((tm, LANES), lambda i: (i, 0))
    res = pl.pallas_call(
        functools.partial(_proj_kernel, kinds=tuple(kinds), scales=scales, want_f32=want_f32, want_bf16=want_bf16),
        grid=(m // tm,),
        in_specs=[pl.BlockSpec((tm, d), lambda i: (i, 0)), pl.BlockSpec((d, n), lambda i: (0, 0)), tab, tab, tab, tab],
        out_specs=specs, out_shape=outs, compiler_params=_cp("parallel"), name="proj",
    )(x_bf, w_bf, *tabs)
    return res


def _diff_finalize(a1, l1, a2, l2, lam_ref, g_ref, lam_init):
    lv = lam_ref[...]
    la = jnp.sum(lv[0:1] * lv[1:2], axis=1, keepdims=True)
    lb = jnp.sum(lv[2:3] * lv[3:4], axis=1, keepdims=True)
    lam = jnp.exp(la) - jnp.exp(lb) + lam_init
    o = a1 / l1 - lam * (a2 / l2)
    o = o * lax.rsqrt(jnp.mean(o * o, axis=-1, keepdims=True) + LN_EPS) * g_ref[...] * (1.0 - lam_init)
    return o


def _diff_attn_kernel(q_ref, k_ref, v_ref, lam_ref, g_ref, o_ref, *, t, lam_init):
    qi = pl.program_id(1)
    q = q_ref[...]
    lane = lax.broadcasted_iota(I32, q.shape, 1)
    zero = jnp.zeros_like(q)
    q1 = jnp.where(lane < A_HALF, q, zero)
    q2 = jnp.where(lane < A_HALF, zero, q)
    rowi = lax.broadcasted_iota(I32, (t, t), 0)
    coli = lax.broadcasted_iota(I32, (t, t), 1)

    def update(s, m, l, a, v):
        mn = jnp.maximum(m, jnp.max(s, axis=-1, keepdims=True))
        al = jnp.exp(m - mn)
        p = jnp.exp(s - mn)
        l = al * l + jnp.sum(p, axis=-1, keepdims=True)
        a = al * a + _dot(p.astype(BF16), v)
        return mn, l, a

    def tile(ti, carry, last):
        m1, l1, a1, m2, l2, a2 = carry
        off = pl.multiple_of(ti * t, t)
        k = k_ref[pl.ds(off, t), :]
        v = v_ref[pl.ds(off, t), :]
        s1 = _dot_nt(q1, k)
        s2 = _dot_nt(q2, k)
        if last:
            s1 = jnp.where(coli <= rowi, s1, NEG)
            s2 = jnp.where(coli <= rowi, s2, NEG)
        m1, l1, a1 = update(s1, m1, l1, a1, v)
        m2, l2, a2 = update(s2, m2, l2, a2, v)
        return m1, l1, a1, m2, l2, a2

    mi = jnp.full((t, 1), NEG, F32)
    li = jnp.zeros((t, 1), F32)
    ai = jnp.zeros((t, HEAD_DIM), F32)
    carry = lax.fori_loop(0, qi, lambda ti, c: tile(ti, c, False), (mi, li, ai, mi, li, ai))
    m1, l1, a1, m2, l2, a2 = tile(qi, carry, True)
    o_ref[...] = _diff_finalize(a1, l1, a2, l2, lam_ref, g_ref, lam_init).astype(o_ref.dtype)


def _diff_attn_prompt(q_bf, k_bf, v_bf, lam_vec, norm_g, lam_init, t=512):
    s = q_bf.shape[0]
    t = min(t, s)
    return pl.pallas_call(
        functools.partial(_diff_attn_kernel, t=t, lam_init=lam_init),
        grid=(A_HEADS, s // t),
        in_specs=[pl.BlockSpec((t, HEAD_DIM), lambda h, i: (i, h)),
                  pl.BlockSpec((s, HEAD_DIM), lambda h, i: (0, h)),
                  pl.BlockSpec((s, HEAD_DIM), lambda h, i: (0, h)),
                  pl.BlockSpec((4, A_HALF), lambda h, i: (0, 0)),
                  pl.BlockSpec((1, HEAD_DIM), lambda h, i: (0, 0))],
        out_specs=pl.BlockSpec((t, HEAD_DIM), lambda h, i: (i, h)),
        out_shape=jax.ShapeDtypeStruct((s, A_WIDTH), BF16),
        compiler_params=_cp("parallel", "parallel"), name="diff_attn",
    )(q_bf, k_bf, v_bf, lam_vec, norm_g.reshape(1, HEAD_DIM))


def _compress_kernel(kc_ref, vc_ref, cos_ref, sin_ref, kcb_ref, vcb_ref, kch, vch, *, nc, steps):
    i = pl.program_id(0)
    rows = kc_ref.shape[0]
    cpb = rows // CMP_STRIDE
    off = pl.multiple_of(i * cpb, cpb)
    kch[pl.ds(off, cpb), :] = jnp.sum(kc_ref[...].reshape(cpb, CMP_STRIDE, KV_WIDTH), axis=1)
    vch[pl.ds(off, cpb), :] = jnp.sum(vc_ref[...].reshape(cpb, CMP_STRIDE, KV_WIDTH), axis=1)

    @pl.when(i == steps - 1)
    def _():
        last = lax.broadcasted_iota(I32, (nc, KV_WIDTH), 0) == nc - 1

        def blocks(ch_ref):
            ch = ch_ref[...]
            nxt = pltpu.roll(ch, nc - 1, 0)
            return jnp.where(last, 0.0, (ch + nxt) / CMP_LEN)

        kb = blocks(kch)
        for c in range(B_KV):
            blk = kb[:, c * LANES:(c + 1) * LANES]
            blk = blk * cos_ref[...] + pltpu.roll(blk, 64, 1) * sin_ref[...]
            kcb_ref[:, c * LANES:(c + 1) * LANES] = blk.astype(BF16)
        vcb_ref[...] = blocks(vch).astype(BF16)


def _compress_prompt(k_f32, v_f32, cos_c, sin_c):
    s = k_f32.shape[0]
    nc = s // CMP_STRIDE
    steps = 4 if s >= 2048 else 1
    rows = s // steps
    return pl.pallas_call(
        functools.partial(_compress_kernel, nc=nc, steps=steps),
        grid=(steps,),
        in_specs=[pl.BlockSpec((rows, KV_WIDTH), lambda i: (i, 2)),
                  pl.BlockSpec((rows, KV_WIDTH), lambda i: (i, 2)),
                  pl.BlockSpec((nc, LANES), lambda i: (0, 0)),
                  pl.BlockSpec((nc, LANES), lambda i: (0, 0))],
        out_specs=[pl.BlockSpec((nc, KV_WIDTH), lambda i: (0, 0)), pl.BlockSpec((nc, KV_WIDTH), lambda i: (0, 0))],
        out_shape=[jax.ShapeDtypeStruct((nc, KV_WIDTH), BF16), jax.ShapeDtypeStruct((nc, KV_WIDTH), BF16)],
        scratch_shapes=[pltpu.VMEM((nc, KV_WIDTH), F32), pltpu.VMEM((nc, KV_WIDTH), F32)],
        compiler_params=_cp("arbitrary"), name="nsa_compress",
    )(k_f32, v_f32, cos_c, sin_c)


NSA_TQ = 128
NSA_TK = 1024


def _softmax_rows(s):
    mx = jnp.max(s, axis=-1, keepdims=True)
    p = jnp.exp(s - mx)
    return p / jnp.sum(p, axis=-1, keepdims=True)


def _nsa_kernel(q_ref, kcb_ref, vcb_ref, ks_ref, vs_ref, kw_ref, vw_ref, g_ref, m_ref, e_ref, o_ref, v_scr, *, nc, nsp):
    b = pl.program_id(1)
    tq, g4 = NSA_TQ, B_GROUP
    q = q_ref[...]
    qs = jnp.concatenate([q[:, g * LANES:(g + 1) * LANES] for g in range(g4)], axis=0)
    pos = b * tq + lax.broadcasted_iota(I32, (tq, 1), 0)

    sc = _dot_nt(qs, kcb_ref[...]).reshape(g4, tq, nc)
    cend = CMP_STRIDE * lax.broadcasted_iota(I32, (1, nc), 1) + (CMP_LEN - 1)
    mc = cend <= pos
    pc = _softmax_rows(jnp.where(mc[None], sc, NEG))
    pc = pc * jnp.where(pos >= CMP_LEN - 1, 1.0, 0.0)[None]
    oc = _dot(pc.reshape(g4 * tq, nc).astype(BF16), vcb_ref[...])

    pcs = pc[0] + pc[1] + pc[2] + pc[3]
    hi, mid, lo = _split3(pcs)
    mm = m_ref[...]
    imp = _dot(hi, mm) + _dot(mid, mm) + _dot(lo, mm)
    imp_t = imp.T
    n_io = lax.broadcasted_iota(I32, (nsp, tq), 0)
    cur_t = lax.shift_right_logical(b * tq + lax.broadcasted_iota(I32, (nsp, tq), 1), 6)
    val = jnp.where(n_io > cur_t, NEG, imp_t)
    forced = jnp.where(n_io == 0, 1, jnp.where(n_io == cur_t, 1, jnp.where(n_io == cur_t - 1, 1, 0)))
    val = jnp.where(forced == 1, BIG, val)
    v_scr[...] = val

    def rank_body(n2, cnt):
        r = v_scr[pl.ds(n2, 1), :]
        beats = jnp.where(r > val, 1, jnp.where(r == val, jnp.where(n2 < n_io, 1, 0), 0))
        return cnt + beats

    cnt = lax.fori_loop(0, 2 * b + 2, rank_body, jnp.zeros((nsp, tq), I32))
    sel_t = jnp.where(cnt < SEL_TOP, jnp.where(n_io <= cur_t, 1.0, 0.0), 0.0)
    sel = sel_t.T.astype(BF16)

    def sel_tile(t, carry):
        m, l, acc = carry
        off = pl.multiple_of(t * NSA_TK, NSA_TK)
        kt = ks_ref[pl.ds(off, NSA_TK), :]
        vt = vs_ref[pl.ds(off, NSA_TK), :]
        s = _dot_nt(qs, kt).reshape(g4, tq, NSA_TK)
        allowed = _dot(sel, e_ref[t])
        kpos = off + lax.broadcasted_iota(I32, (tq, NSA_TK), 1)
        ok = jnp.where(kpos <= pos, allowed, 0.0) > 0.5
        s = jnp.where(ok[None], s, NEG)
        mn = jnp.maximum(m, jnp.max(s, axis=-1, keepdims=True))
        al = jnp.exp(m - mn)
        p = jnp.exp(s - mn)
        l = al * l + jnp.sum(p, axis=-1, keepdims=True)
        pv = _dot(p.reshape(g4 * tq, NSA_TK).astype(BF16), vt).reshape(g4, tq, HEAD_DIM)
        return mn, l, al * acc + pv

    ntiles = (b * tq + tq + NSA_TK - 1) // NSA_TK
    m0 = jnp.full((g4, tq, 1), NEG, F32)
    l0 = jnp.zeros((g4, tq, 1), F32)
    a0 = jnp.zeros((g4, tq, HEAD_DIM), F32)
    _, ls, accs = lax.fori_loop(0, ntiles, sel_tile, (m0, l0, a0))
    osel = accs / ls

    wlen = WINDOW + tq
    wstart = pl.multiple_of(jnp.maximum(b * tq - WINDOW, 0), tq)
    kwt = kw_ref[pl.ds(wstart, wlen), :]
    vwt = vw_ref[pl.ds(wstart, wlen), :]
    sw = _dot_nt(qs, kwt).reshape(g4, tq, wlen)
    kp = wstart + lax.broadcasted_iota(I32, (tq, wlen), 1)
    okw = jnp.where(kp <= pos, jnp.where(kp > pos - WINDOW, 1, 0), 0) == 1
    pw = _softmax_rows(jnp.where(okw[None], sw, NEG))
    ow = _dot(pw.reshape(g4 * tq, wlen).astype(BF16), vwt).reshape(g4, tq, HEAD_DIM)

    oc = oc.reshape(g4, tq, HEAD_DIM)
    gt = g_ref[...]
    for g in range(g4):
        og = gt[:, 3 * g:3 * g + 1] * oc[g] + gt[:, 3 * g + 1:3 * g + 2] * osel[g] + gt[:, 3 * g + 2:3 * g + 3] * ow[g]
        o_ref[:, g * LANES:(g + 1) * LANES] = og.astype(o_ref.dtype)


def _nsa_tables(s):
    nc = s // CMP_STRIDE
    ns = s // SEL_LEN
    nsp = -(-ns // LANES) * LANES
    j = jnp.arange(nc)[:, None]
    n = jnp.arange(nsp)[None, :]
    pool = ((j >= 4 * n - 1) & (j <= 4 * n + 3) & (n < ns) & (j < nc - 1)).astype(BF16)
    t = jnp.arange(s)[None, :]
    expand = ((t // SEL_LEN) == jnp.arange(nsp)[:, None]).astype(BF16)
    expand = expand.reshape(nsp, s // NSA_TK, NSA_TK).transpose(1, 0, 2)
    return pool, expand


def _nsa_prompt(q_bf, kcb, vcb, k_bf, v_bf, w_bf, gates, pool, expand):
    s = q_bf.shape[0]
    nc = s // CMP_STRIDE
    nsp = pool.shape[1]
    nt = s // NSA_TK
    full = lambda col: pl.BlockSpec((s, HEAD_DIM), col)
    return pl.pallas_call(
        functools.partial(_nsa_kernel, nc=nc, nsp=nsp),
        grid=(B_KV, s // NSA_TQ),
        in_specs=[pl.BlockSpec((NSA_TQ, B_GROUP * HEAD_DIM), lambda k, b: (b, 1 + k)),
                  pl.BlockSpec((nc, HEAD_DIM), lambda k, b: (0, k)),
                  pl.BlockSpec((nc, HEAD_DIM), lambda k, b: (0, k)),
                  full(lambda k, b: (0, 6 + k)), full(lambda k, b: (0, 6 + k)),
                  full(lambda k, b: (0, k)), full(lambda k, b: (0, 2 + k)),
                  pl.BlockSpec((NSA_TQ, LANES), lambda k, b: (b, k)),
                  pl.BlockSpec((nc, nsp), lambda k, b: (0, 0)),
                  pl.BlockSpec((nt, nsp, NSA_TK), lambda k, b: (0, 0, 0))],
        out_specs=pl.BlockSpec((NSA_TQ, B_GROUP * HEAD_DIM), lambda k, b: (b, k)),
        out_shape=jax.ShapeDtypeStruct((s, B_WIDTH), BF16),
        scratch_shapes=[pltpu.VMEM((nsp, NSA_TQ), F32)],
        compiler_params=_cp("parallel", "arbitrary"), name="nsa",
    )(q_bf, kcb, vcb, k_bf, v_bf, w_bf, w_bf, gates, pool, expand)


def _pool_kernel(u_ref, up_ref, pw_ref, ps_ref, o_ref, ext, *, tm):
    i = pl.program_id(0)
    cur = u_ref[...]
    prev = jnp.where(i > 0, up_ref[...], 0.0)
    ext[0:16, :] = prev
    ext[16:16 + tm, :] = cur
    pos = i * tm + lax.broadcasted_iota(I32, (tm, 1), 0)
    for g, w in enumerate(C_WINDOWS):
        cols = slice(g * C_GW, (g + 1) * C_GW)
        acc = cur[:, cols]
        for r in range(1, w):
            acc = acc + ext[16 - r:16 - r + tm, cols]
        cnt = jnp.minimum(pos + 1, w).astype(F32)
        d = acc / cnt - cur[:, cols]
        o = _dot(d.astype(BF16), pw_ref[g]) * ps_ref[:, cols]
        o_ref[:, cols] = o.astype(o_ref.dtype)


def _pool_prompt(u, pool_w_bf, pool_scale, tm=512):
    s = u.shape[0]
    tm = min(tm, s)
    return pl.pallas_call(
        functools.partial(_pool_kernel, tm=tm),
        grid=(s // tm,),
        in_specs=[pl.BlockSpec((tm, C_WIDTH), lambda i: (i, 0)),
                  pl.BlockSpec((16, C_WIDTH), lambda i: (jnp.maximum(i * (tm // 16) - 1, 0), 0)),
                  pl.BlockSpec((4, C_GW, C_GW), lambda i: (0, 0, 0)),
                  pl.BlockSpec((1, C_WIDTH), lambda i: (0, 0))],
        out_specs=pl.BlockSpec((tm, C_WIDTH), lambda i: (i, 0)),
        out_shape=jax.ShapeDtypeStruct((s, C_WIDTH), BF16),
        scratch_shapes=[pltpu.VMEM((tm + 16, C_WIDTH), F32)],
        compiler_params=_cp("parallel"), name="pool_mixer",
    )(u, u, pool_w_bf, pool_scale.reshape(1, C_WIDTH))


def _ln(y, g, b):
    mu = jnp.mean(y, axis=-1, keepdims=True)
    yc = y - mu
    var = jnp.mean(yc * yc, axis=-1, keepdims=True)
    return yc * lax.rsqrt(var + LN_EPS) * g + b


def _outproj_kernel(oa_ref, ob_ref, oc_ref, x_ref, w_ref, g_ref, b_ref, of_ref, ob16_ref, *, alpha):
    h = _dot(oa_ref[...], w_ref[0:A_WIDTH, :])
    h = h + _dot(ob_ref[...], w_ref[A_WIDTH:A_WIDTH + B_WIDTH, :])
    h = h + _dot(oc_ref[...], w_ref[A_WIDTH + B_WIDTH:, :])
    y = _ln(alpha * x_ref[...] + h, g_ref[...], b_ref[...])
    of_ref[...] = y
    ob16_ref[...] = y.astype(BF16)


def _outproj_ln(oa, ob, oc, x, w_bf, g, b, alpha, tm=256):
    m, d = x.shape
    tm = min(tm, m)
    row = lambda n: pl.BlockSpec((tm, n), lambda i: (i, 0))
    vec = pl.BlockSpec((1, d), lambda i: (0, 0))
    return pl.pallas_call(
        functools.partial(_outproj_kernel, alpha=alpha),
        grid=(m // tm,),
        in_specs=[row(A_WIDTH), row(B_WIDTH), row(C_WIDTH), row(d), pl.BlockSpec((d, d), lambda i: (0, 0)), vec, vec],
        out_specs=[row(d), row(d)],
        out_shape=[jax.ShapeDtypeStruct((m, d), F32), jax.ShapeDtypeStruct((m, d), BF16)],
        compiler_params=_cp("parallel"), name="outproj_ln",
    )(oa, ob, oc, x, w_bf, g.reshape(1, d), b.reshape(1, d))


def _ffn_kernel(xb_ref, x_ref, gate_ref, wg_ref, wu_ref, wd_ref, g_ref, b_ref, of_ref, ob16_ref, acc, *, alpha, gated):
    e, f = pl.program_id(1), pl.program_id(2)

    @pl.when((e == 0) & (f == 0))
    def _():
        acc[...] = jnp.zeros_like(acc)

    xb = xb_ref[...]
    hg = _dot(xb, wg_ref[...].astype(BF16))
    hu = _dot(xb, wu_ref[...].astype(BF16))
    h = hg * (1.0 / (1.0 + jnp.exp(-hg))) * hu
    y = _dot(h.astype(BF16), wd_ref[...].astype(BF16))
    if gated:
        y = gate_ref[...] * y
    acc[...] += y

    @pl.when((e == pl.num_programs(1) - 1) & (f == pl.num_programs(2) - 1))
    def _():
        o = _ln(alpha * x_ref[...] + acc[...], g_ref[...], b_ref[...])
        of_ref[...] = o
        ob16_ref[...] = o.astype(BF16)


def _ffn_ln(xb, x, gate3, wg, wu, wd, g, b, alpha, gated, tm=512, tf=512):
    m, d = x.shape
    ne, _, ff = wg.shape
    tm = min(tm, m)
    vec = pl.BlockSpec((1, d), lambda i, e, f: (0, 0))
    return pl.pallas_call(
        functools.partial(_ffn_kernel, alpha=alpha, gated=gated),
        grid=(m // tm, ne, ff // tf),
        in_specs=[pl.BlockSpec((tm, d), lambda i, e, f: (i, 0)),
                  pl.BlockSpec((tm, d), lambda i, e, f: (i, 0)),
                  pl.BlockSpec((None, tm, 1), lambda i, e, f: (e, i, 0)),
                  pl.BlockSpec((None, d, tf), lambda i, e, f: (e, 0, f)),
                  pl.BlockSpec((None, d, tf), lambda i, e, f: (e, 0, f)),
                  pl.BlockSpec((None, tf, d), lambda i, e, f: (e, f, 0)),
                  vec, vec],
        out_specs=[pl.BlockSpec((tm, d), lambda i, e, f: (i, 0)), pl.BlockSpec((tm, d), lambda i, e, f: (i, 0))],
        out_shape=[jax.ShapeDtypeStruct((m, d), F32), jax.ShapeDtypeStruct((m, d), BF16)],
        scratch_shapes=[pltpu.VMEM((tm, d), F32)],
        compiler_params=_cp("parallel", "arbitrary", "arbitrary"), name="ffn_ln",
    )(xb, x, gate3, wg, wu, wd, g.reshape(1, d), b.reshape(1, d))


PROJ_SIZES = (A_WIDTH, A_WIDTH, A_WIDTH, B_WIDTH, KV_WIDTH, KV_WIDTH, KV_WIDTH, KV_WIDTH, KV_WIDTH, KV_WIDTH, 3 * B_HEADS, C_WIDTH)
K_KINDS = ("r64",) * 4 + ("none",) * 2 + ("r128",) * 2
V_KINDS = ("none",) * 8
W_KINDS = ("r128",) * 2 + ("none",) * 2
Q_KINDS = ("r64",) * 4 + ("r128",) * 8
Q_SCALES = (A_HALF ** -0.5,) * 4 + (HEAD_DIM ** -0.5,) * 8


def _split_w_in(w):
    parts, o = [], 0
    for n in PROJ_SIZES:
        parts.append(w[:, o:o + n])
        o += n
    qa, ka, va, qb, kc, vc, ks, vs, kw, vw, gb, u = parts
    d = w.shape[0]
    per_kv = 3 * B_GROUP
    zpad = jnp.zeros((d, LANES - per_kv), w.dtype)
    wg = jnp.concatenate([gb[:, :per_kv], zpad, gb[:, per_kv:], zpad], axis=1)
    cat = lambda *xs: jnp.concatenate(xs, axis=1).astype(BF16)
    return dict(k=cat(ka, kc, ks), v=cat(va, vc, vs), w=cat(kw, vw), q=cat(qa, qb), u=u.astype(BF16), g=wg.astype(BF16))


def _project_all(x_bf, w, tabs, tm=512):
    kf, kb = _proj(x_bf, w["k"], tabs, K_KINDS, tm=tm)
    vf, vb = _proj(x_bf, w["v"], tabs, V_KINDS, tm=tm)
    wf, wb = _proj(x_bf, w["w"], tabs, W_KINDS, tm=tm)
    (qb,) = _proj(x_bf, w["q"], tabs, Q_KINDS, scales=Q_SCALES, want_f32=False, tm=tm)
    (uf,) = _proj(x_bf, w["u"], tabs, ("none",) * 4, want_bf16=False, tm=tm)
    (gf,) = _proj(x_bf, w["g"], tabs, ("sig",) * 2, want_bf16=False, tm=tm)
    return kf, kb, vf, vb, wf, wb, qb, uf, gf


def _prompt_mixer(x, x_bf, w, w_out_bf, lam_vec, lam_init, norm_g, pool_w_bf, pool_scale, ln_g, ln_b, alpha, consts):
    s = x.shape[0]
    kf, kb, vf, vb, wf, wb, qb, uf, gf = _project_all(x_bf, w, consts["tabs"])
    oa = _diff_attn_prompt(qb, kb, vb, lam_vec, norm_g, lam_init)
    kcb, vcb = _compress_prompt(kf, vf, consts["cos_c"], consts["sin_c"])
    ob = _nsa_prompt(qb, kcb, vcb, kb, vb, wb, gf, consts["pool"], consts["expand"])
    oc = _pool_prompt(uf, pool_w_bf, pool_scale)
    xn, xn_bf = _outproj_ln(oa, ob, oc, x, w_out_bf, ln_g, ln_b, alpha)
    wbuf = min(WINDOW, s)
    return xn, xn_bf, kf, vf, wf[s - wbuf:], uf[s - POOL_HIST:]


def _prompt_consts(s):
    pos = jnp.arange(s, dtype=I32)
    tabs = _rope_tables(pos)
    nc = s // CMP_STRIDE
    cend = jnp.arange(nc, dtype=I32) * CMP_STRIDE + (CMP_LEN - 1)
    cos_c, sin_c, _, _ = _rope_tables(cend)
    pool, expand = _nsa_tables(s)
    return dict(tabs=tabs, cos_c=cos_c, sin_c=sin_c, pool=pool, expand=expand)


def _router_kernel(x_ref, rw_ref, gate_ref, idx_ref, gv_ref):
    xh, xm, xl = _split3(x_ref[...])
    wh, wm, wl = _split3(rw_ref[...])
    logits = _dot(xh, wh) + (_dot(xh, wm) + _dot(xm, wh)) + (_dot(xh, wl) + _dot(xm, wm) + _dot(xl, wh))
    lane = lax.broadcasted_iota(I32, logits.shape, 1).astype(F32)
    ninf = -jnp.inf
    lg = jnp.where(lane < N_EXPERTS, logits, ninf)
    m1 = jnp.max(lg, axis=-1, keepdims=True)
    i1 = jnp.min(jnp.where(lg == m1, lane, float(LANES)), axis=-1, keepdims=True)
    lg2 = jnp.where(lane == i1, ninf, lg)
    m2 = jnp.max(lg2, axis=-1, keepdims=True)
    i2 = jnp.min(jnp.where(lg2 == m2, lane, float(LANES)), axis=-1, keepdims=True)
    e2 = jnp.exp(m2 - m1)
    den = 1.0 + e2
    g1 = 1.0 / den
    g2 = e2 / den
    gate_ref[...] = jnp.where(lane == i1, g1, 0.0) + jnp.where(lane == i2, g2, 0.0)
    idx_ref[...] = jnp.where(lane == 0.0, i1, jnp.where(lane == 1.0, i2, 0.0)).astype(I32)
    gv_ref[...] = jnp.where(lane == 0.0, g1, jnp.where(lane == 1.0, g2, 0.0))


def _router(x, router_w, tm=512):
    m, d = x.shape
    tm = min(tm, m)
    rw = jnp.concatenate([router_w, jnp.zeros((d, LANES - N_EXPERTS), F32)], axis=1)
    blk = pl.BlockSpec((tm, LANES), lambda i: (i, 0))
    return pl.pallas_call(
        _router_kernel, grid=(m // tm,),
        in_specs=[pl.BlockSpec((tm, d), lambda i: (i, 0)), pl.BlockSpec((d, LANES), lambda i: (0, 0))],
        out_specs=[blk, blk, blk],
        out_shape=[jax.ShapeDtypeStruct((m, LANES), F32), jax.ShapeDtypeStruct((m, LANES), I32), jax.ShapeDtypeStruct((m, LANES), F32)],
        compiler_params=_cp("parallel"), name="router",
    )(x, rw)


MOE_TM = 512


def _route_tables(ids, gvals, tm):
    s = ids.shape[0]
    na = 2 * s
    flat_e = ids.reshape(-1)
    onehot = (flat_e[:, None] == jnp.arange(N_EXPERTS, dtype=I32)[None, :]).astype(I32)
    csum = jnp.cumsum(onehot, axis=0)
    rank = jnp.take_along_axis(csum, flat_e[:, None], axis=1)[:, 0] - 1
    counts = csum[-1]
    pcount = ((counts + tm - 1) // tm) * tm
    ends = jnp.cumsum(pcount)
    offs = ends - pcount
    dest = (offs[flat_e] + rank).astype(I32)
    nrows = na + N_EXPERTS * tm
    ntiles = nrows // tm
    src = jnp.zeros((nrows,), I32).at[dest].set(jnp.arange(na, dtype=I32) // 2)
    gsort = jnp.zeros((nrows,), F32).at[dest].set(gvals.reshape(-1))
    tstart = jnp.arange(ntiles, dtype=I32) * tm
    te = jnp.sum((tstart[:, None] >= ends[None, :]).astype(I32), axis=1)
    valid = (te < N_EXPERTS).astype(I32)
    last_e = jnp.max(jnp.where(pcount > 0, jnp.arange(N_EXPERTS, dtype=I32), 0))
    te = jnp.where(valid == 1, te, last_e).astype(I32)
    return src, gsort.reshape(nrows, 1), te, valid, dest


def _row_copy(src_hbm, row, dst, r, sem):
    return pltpu.make_async_copy(src_hbm.at[pl.ds(row, 1), :], dst.at[pl.ds(r, 1), :], sem)


def _moe_kernel(src_ref, te_ref, valid_ref, x_hbm, gs_ref, wg_ref, wu_ref, wd_ref, y_ref, xbuf, xb, sem, *, tm):
    i, f = pl.program_id(0), pl.program_id(1)

    @pl.when(f == 0)
    def _():
        base = i * tm

        def issue(r, c):
            _row_copy(x_hbm, src_ref[base + r], xbuf, r, sem).start()
            return c

        def wait(r, c):
            _row_copy(x_hbm, 0, xbuf, r, sem).wait()
            return c

        lax.fori_loop(0, tm, issue, 0)
        lax.fori_loop(0, tm, wait, 0)
        xb[...] = xbuf[...].astype(BF16)
        y_ref[...] = jnp.zeros_like(y_ref)

    @pl.when(valid_ref[i] == 1)
    def _():
        x = xb[...]
        hg = _dot(x, wg_ref[...].astype(BF16))
        hu = _dot(x, wu_ref[...].astype(BF16))
        h = hg * (1.0 / (1.0 + jnp.exp(-hg))) * hu
        y_ref[...] += _dot(h.astype(BF16), wd_ref[...].astype(BF16))

    @pl.when(f == pl.num_programs(1) - 1)
    def _():
        y_ref[...] = gs_ref[...] * y_ref[...]


def _moe_grouped(x, src, gsort, te, valid, wg, wu, wd, tm, tf=512):
    d = x.shape[1]
    ff = wg.shape[2]
    nrows = gsort.shape[0]
    grid_spec = pltpu.PrefetchScalarGridSpec(
        num_scalar_prefetch=3, grid=(nrows // tm, ff // tf),
        in_specs=[pl.BlockSpec(memory_space=pl.ANY),
                  pl.BlockSpec((tm, 1), lambda i, f, s_, e_, v_: (i, 0)),
                  pl.BlockSpec((None, d, tf), lambda i, f, s_, e_, v_: (e_[i], 0, f)),
                  pl.BlockSpec((None, d, tf), lambda i, f, s_, e_, v_: (e_[i], 0, f)),
                  pl.BlockSpec((None, tf, d), lambda i, f, s_, e_, v_: (e_[i], f, 0))],
        out_specs=pl.BlockSpec((tm, d), lambda i, f, s_, e_, v_: (i, 0)),
        scratch_shapes=[pltpu.VMEM((tm, d), F32), pltpu.VMEM((tm, d), BF16), pltpu.SemaphoreType.DMA(())])
    return pl.pallas_call(
        functools.partial(_moe_kernel, tm=tm), grid_spec=grid_spec,
        out_shape=jax.ShapeDtypeStruct((nrows, d), F32),
        compiler_params=_cp("arbitrary", "arbitrary"), name="moe_grouped",
    )(src, te, valid, x, gsort, wg, wu, wd)


def _combine_kernel(dest_ref, y_hbm, x_ref, g_ref, b_ref, of_ref, ob16_ref, buf, sem, *, tm, alpha):
    base = 2 * pl.program_id(0) * tm

    def issue(t, c):
        _row_copy(y_hbm, dest_ref[base + 2 * t], buf.at[0], t, sem).start()
        _row_copy(y_hbm, dest_ref[base + 2 * t + 1], buf.at[1], t, sem).start()
        return c

    def wait(t, c):
        _row_copy(y_hbm, 0, buf.at[0], t, sem).wait()
        _row_copy(y_hbm, 0, buf.at[1], t, sem).wait()
        return c

    lax.fori_loop(0, tm, issue, 0)
    lax.fori_loop(0, tm, wait, 0)
    o = _ln(alpha * x_ref[...] + (buf[0] + buf[1]), g_ref[...], b_ref[...])
    of_ref[...] = o
    ob16_ref[...] = o.astype(BF16)


def _moe_combine_ln(y_sorted, dest, x, g, b, alpha, tm=256):
    m, d = x.shape
    row = pl.BlockSpec((tm, d), lambda i, d_: (i, 0))
    vec = pl.BlockSpec((1, d), lambda i, d_: (0, 0))
    grid_spec = pltpu.PrefetchScalarGridSpec(
        num_scalar_prefetch=1, grid=(m // tm,),
        in_specs=[pl.BlockSpec(memory_space=pl.ANY), row, vec, vec],
        out_specs=[row, row],
        scratch_shapes=[pltpu.VMEM((2, tm, d), F32), pltpu.SemaphoreType.DMA(())])
    return pl.pallas_call(
        functools.partial(_combine_kernel, tm=tm, alpha=alpha), grid_spec=grid_spec,
        out_shape=[jax.ShapeDtypeStruct((m, d), F32), jax.ShapeDtypeStruct((m, d), BF16)],
        compiler_params=_cp("arbitrary"), name="moe_combine_ln",
    )(dest, y_sorted, x, g.reshape(1, d), b.reshape(1, d))


def _moe_prompt(xn, router_w, wg, wu, wd, g, b, alpha):
    _, idx, gv = _router(xn, router_w)
    src, gsort, te, valid, dest = _route_tables(idx[:, :TOP_K], gv[:, :TOP_K], MOE_TM)
    y_sorted = _moe_grouped(xn, src, gsort, te, valid, wg, wu, wd, MOE_TM)
    return _moe_combine_ln(y_sorted, dest, xn, g, b, alpha)


def _head_rows(row, width=HEAD_DIM, reps=2, heads=A_HEADS, first=0):
    parts = []
    for h in range(heads):
        parts += [row[:, (first + h) * width:(first + h + 1) * width]] * reps
    return jnp.concatenate(parts, axis=0)


SDIFF_PAGES = 8


def _sdiff_kernel(pt_ref, q_ref, *refs, lam_init, npg):
    page_refs = refs[:2 * npg]
    kn_ref, vn_ref, lam_ref, g_ref, o_ref, kch_ref, vch_ref, qm, m_s, l_s, acc = refs[2 * npg:]
    p = pl.program_id(1)
    nq = 2 * A_HEADS
    rows = lax.broadcasted_iota(I32, (nq, HEAD_DIM), 0)
    lane = lax.broadcasted_iota(I32, (nq, HEAD_DIM), 1)

    @pl.when(p == 0)
    def _():
        q8 = _head_rows(q_ref[...].astype(F32))
        qm[...] = jnp.where((lane // A_HALF) == (rows % 2), q8, 0.0)
        m_s[...] = jnp.full_like(m_s, NEG)
        l_s[...] = jnp.zeros_like(l_s)
        acc[...] = jnp.zeros_like(acc)

    q8 = qm[...]
    q8b = q8.astype(BF16)
    nchunk = PAGE_SIZE // CMP_STRIDE
    flat = PAGE_SIZE * CACHE_HEADS
    srow = lax.broadcasted_iota(I32, (nq, flat), 0)
    scol = lax.broadcasted_iota(I32, (nq, flat), 1)
    own = (scol % CACHE_HEADS) == (srow // 2)
    s_parts, v_flat = [], []
    for j in range(npg):
        kp_ref, vp_ref = page_refs[j], page_refs[npg + j]
        kp = kp_ref[...]
        vp = vp_ref[...]
        s_parts.append(jnp.where(own, _dot_nt(q8b, kp.reshape(flat, HEAD_DIM).astype(BF16)), NEG))
        v_flat.append(vp.reshape(flat, HEAD_DIM).astype(BF16))
        kch_ref[j * nchunk:(j + 1) * nchunk] = jnp.sum(kp.reshape(nchunk, CMP_STRIDE, CACHE_HEADS, HEAD_DIM), axis=1)
        vch_ref[j * nchunk:(j + 1) * nchunk] = jnp.sum(vp.reshape(nchunk, CMP_STRIDE, CACHE_HEADS, HEAD_DIM), axis=1)
    s = jnp.concatenate(s_parts, axis=1)
    mn = jnp.maximum(m_s[...], jnp.max(s, axis=-1, keepdims=True))
    al = jnp.exp(m_s[...] - mn)
    pr = jnp.exp(s - mn)
    l_s[...] = al * l_s[...] + jnp.sum(pr, axis=-1, keepdims=True)
    prb = pr.astype(BF16)
    pv = _dot(prb[:, 0:flat], v_flat[0])
    for j in range(1, npg):
        pv = pv + _dot(prb[:, j * flat:(j + 1) * flat], v_flat[j])
    acc[...] = al * acc[...] + pv
    m_s[...] = mn

    @pl.when(p == pl.num_programs(1) - 1)
    def _():
        k8 = _head_rows(kn_ref[...])
        v8 = _head_rows(vn_ref[...])
        sn = jnp.sum(q8 * k8, axis=-1, keepdims=True)
        m2 = jnp.maximum(m_s[...], sn)
        a2 = jnp.exp(m_s[...] - m2)
        pn = jnp.exp(sn - m2)
        lt = a2 * l_s[...] + pn
        o8 = (a2 * acc[...] + pn * v8) / lt
        lv = lam_ref[...]
        la = jnp.sum(lv[0:1] * lv[1:2], axis=1, keepdims=True)
        lb = jnp.sum(lv[2:3] * lv[3:4], axis=1, keepdims=True)
        lam = jnp.exp(la) - jnp.exp(lb) + lam_init
        outs = []
        for h in range(A_HEADS):
            o = o8[2 * h:2 * h + 1] - lam * o8[2 * h + 1:2 * h + 2]
            o = o * lax.rsqrt(jnp.mean(o * o, axis=-1, keepdims=True) + LN_EPS) * g_ref[...] * (1.0 - lam_init)
            outs.append(o)
        o_ref[...] = jnp.concatenate(outs, axis=1).astype(o_ref.dtype)


def _sample_diff(layer, pt, q3, cache_k, cache_v, kn3, vn3, lam_vec, norm_g, lam_init):
    nb, pages = pt.shape
    npg = math.gcd(SDIFF_PAGES, pages)
    nchunk = PAGE_SIZE // CMP_STRIDE

    def page_spec(j):
        return pl.BlockSpec((None, None, PAGE_SIZE, CACHE_HEADS, HEAD_DIM),
                            lambda b, p, t: (layer, t[b, p * npg + j], 0, 0, 0))

    row = lambda n: pl.BlockSpec((None, 1, n), lambda b, p, t: (b, 0, 0))
    grid_spec = pltpu.PrefetchScalarGridSpec(
        num_scalar_prefetch=1, grid=(nb, pages // npg),
        in_specs=[row(A_WIDTH)] + [page_spec(j) for j in range(npg)] * 2 + [
                  row(A_WIDTH), row(A_WIDTH),
                  pl.BlockSpec((4, A_HALF), lambda b, p, t: (0, 0)),
                  pl.BlockSpec((1, HEAD_DIM), lambda b, p, t: (0, 0))],
        out_specs=[row(A_WIDTH),
                   pl.BlockSpec((None, npg * nchunk, CACHE_HEADS, HEAD_DIM), lambda b, p, t: (b, p, 0, 0)),
                   pl.BlockSpec((None, npg * nchunk, CACHE_HEADS, HEAD_DIM), lambda b, p, t: (b, p, 0, 0))],
        scratch_shapes=[pltpu.VMEM((2 * A_HEADS, HEAD_DIM), F32), pltpu.VMEM((2 * A_HEADS, 1), F32),
                        pltpu.VMEM((2 * A_HEADS, 1), F32), pltpu.VMEM((2 * A_HEADS, HEAD_DIM), F32)])
    chunk_sums = jax.ShapeDtypeStruct((nb, pages * nchunk, CACHE_HEADS, HEAD_DIM), F32)
    return pl.pallas_call(
        functools.partial(_sdiff_kernel, lam_init=lam_init, npg=npg), grid_spec=grid_spec,
        out_shape=[jax.ShapeDtypeStruct((nb, 1, A_WIDTH), BF16), chunk_sums, chunk_sums],
        compiler_params=_cp("parallel", "arbitrary"), name="sample_diff",
    )(pt, q3, *([cache_k] * npg), *([cache_v] * npg), kn3, vn3, lam_vec, norm_g.reshape(1, HEAD_DIM))


def _group_rows(q_row):
    q4 = jnp.concatenate([q_row[:, g * HEAD_DIM:(g + 1) * HEAD_DIM] for g in range(B_GROUP)], axis=0)
    return jnp.concatenate([q4, jnp.zeros_like(q4)], axis=0)


def _snsa_kernel(q0_ref, q1_ref, kch_ref, vch_ref, kn_ref, vn_ref, cos_ref, sin_ref, m_ref, st_ref, wn_ref, g_ref,
                 idx_ref, ocw_ref, nw_ref, *, pos, nc, nsp):
    rowc = lax.broadcasted_iota(I32, (nc, KV_WIDTH), 0)
    kn = kn_ref[...]
    vn = vn_ref[...]
    c0 = A_WIDTH

    def blocks(ch, new_row):
        nxt = jnp.where(rowc == nc - 1, new_row, pltpu.roll(ch, nc - 1, 0))
        return (ch + nxt) / CMP_LEN

    def cmp_heads(ref):
        return jnp.concatenate([ref[:, A_HEADS + c, :] for c in range(B_KV)], axis=1)

    kb = blocks(cmp_heads(kch_ref), kn[:, c0:c0 + KV_WIDTH])
    vcb = blocks(cmp_heads(vch_ref), vn[:, c0:c0 + KV_WIDTH]).astype(BF16)
    kcb = []
    for c in range(B_KV):
        blk = kb[:, c * LANES:(c + 1) * LANES]
        kcb.append((blk * cos_ref[...] + pltpu.roll(blk, 64, 1) * sin_ref[...]).astype(BF16))

    cend = CMP_STRIDE * lax.broadcasted_iota(I32, (8, nc), 1) + (CMP_LEN - 1)
    mc = cend <= pos
    any_valid = 1.0 if pos >= CMP_LEN - 1 else 0.0
    cur = pos // SEL_LEN
    st = st_ref[...]
    wb = st.shape[0]
    wn = wn_ref[...]
    gt = g_ref[...]
    wcol = lax.broadcasted_iota(I32, (8, wb), 1)
    okw = (pos - wb + wcol) > pos - WINDOW
    n_row = lax.broadcasted_iota(I32, (1, nsp), 1)
    ii = lax.broadcasted_iota(I32, (nsp, nsp), 0)
    jj = lax.broadcasted_iota(I32, (nsp, nsp), 1)
    rr = lax.broadcasted_iota(I32, (SEL_TOP, nsp), 0).astype(F32)
    nn = lax.broadcasted_iota(I32, (SEL_TOP, nsp), 1).astype(F32)
    outs = []
    for k, q_ref in enumerate((q0_ref, q1_ref)):
        qs = _group_rows(q_ref[...])
        sc = _dot_nt(qs, kcb[k])
        pc = _softmax_rows(jnp.where(mc, sc, NEG)) * any_valid
        oc = _dot(pc.astype(BF16), vcb[:, k * LANES:(k + 1) * LANES])
        pcs = pc[0:1] + pc[1:2] + pc[2:3] + pc[3:4]
        hi, mid, lo = _split3(jnp.broadcast_to(pcs, (8, nc)))
        mm = m_ref[...]
        imp = (_dot(hi, mm) + _dot(mid, mm) + _dot(lo, mm))[0:1]
        val = jnp.where(n_row > cur, NEG, imp)
        forced = jnp.where(n_row == 0, 1, jnp.where(n_row == cur, 1, jnp.where(n_row == cur - 1, 1, 0)))
        val = jnp.where(forced == 1, BIG, val)
        a_m = jnp.broadcast_to(val, (nsp, nsp))
        b_m = a_m.T
        beats = jnp.where(b_m > a_m, 1.0, jnp.where(b_m == a_m, jnp.where(ii < jj, 1.0, 0.0), 0.0))
        rank = jnp.sum(beats, axis=0, keepdims=True)
        hit = jnp.broadcast_to(rank, (SEL_TOP, nsp)) == rr
        picked = jnp.sum(jnp.where(hit, nn, 0.0), axis=1, keepdims=True)
        idx_ref[k] = jnp.broadcast_to(picked, (SEL_TOP, LANES)).astype(I32)

        kw_s = st[:, k * LANES:(k + 1) * LANES].astype(BF16)
        vw_s = st[:, (B_KV + k) * LANES:(B_KV + k + 1) * LANES].astype(BF16)
        sw = jnp.where(okw, _dot_nt(qs, kw_s), NEG)
        qf = qs.astype(F32)
        s_new = jnp.sum(qf * wn[:, k * LANES:(k + 1) * LANES], axis=-1, keepdims=True)
        mw = jnp.maximum(jnp.max(sw, axis=-1, keepdims=True), s_new)
        pw = jnp.exp(sw - mw)
        pn = jnp.exp(s_new - mw)
        ow = (_dot(pw.astype(BF16), vw_s) + pn * wn[:, (B_KV + k) * LANES:(B_KV + k + 1) * LANES]) / (jnp.sum(pw, axis=-1, keepdims=True) + pn)
        for g in range(B_GROUP):
            c = k * LANES + 3 * g
            outs.append(gt[:, c:c + 1] * oc[g:g + 1] + gt[:, c + 2:c + 3] * ow[g:g + 1])
    ocw_ref[...] = jnp.concatenate(outs, axis=1)
    roww = lax.broadcasted_iota(I32, st.shape, 0)
    nw_ref[...] = jnp.where(roww == wb - 1, wn, pltpu.roll(st, wb - 1, 0))


def _sample_nsa_a(q3, kch, vch, kn3, vn3, cos_c, sin_c, pool, st, wn3, g3, pos):
    nb = q3.shape[0]
    nc = kch.shape[1]
    nsp = pool.shape[1]
    wb = st.shape[1]
    b3 = lambda n, c=0: pl.BlockSpec((None, 1, n), lambda b: (b, 0, c))
    return pl.pallas_call(
        functools.partial(_snsa_kernel, pos=pos, nc=nc, nsp=nsp),
        grid=(nb,),
        in_specs=[b3(4 * HEAD_DIM, 1), b3(4 * HEAD_DIM, 2),
                  pl.BlockSpec((None, nc, CACHE_HEADS, HEAD_DIM), lambda b: (b, 0, 0, 0)),
                  pl.BlockSpec((None, nc, CACHE_HEADS, HEAD_DIM), lambda b: (b, 0, 0, 0)),
                  b3(8 * HEAD_DIM), b3(8 * HEAD_DIM),
                  pl.BlockSpec((nc, LANES), lambda b: (0, 0)), pl.BlockSpec((nc, LANES), lambda b: (0, 0)),
                  pl.BlockSpec((nc, nsp), lambda b: (0, 0)),
                  pl.BlockSpec((None, wb, 4 * HEAD_DIM), lambda b: (b, 0, 0)),
                  b3(4 * HEAD_DIM), b3(2 * LANES)],
        out_specs=[pl.BlockSpec((None, B_KV, SEL_TOP, LANES), lambda b: (b, 0, 0, 0)),
                   b3(B_WIDTH),
                   pl.BlockSpec((None, wb, 4 * HEAD_DIM), lambda b: (b, 0, 0))],
        out_shape=[jax.ShapeDtypeStruct((nb, B_KV, SEL_TOP, LANES), I32),
                   jax.ShapeDtypeStruct((nb, 1, B_WIDTH), F32),
                   jax.ShapeDtypeStruct((nb, wb, 4 * HEAD_DIM), F32)],
        compiler_params=_cp("parallel"), name="sample_nsa_a",
    )(q3, q3, kch, vch, kn3, vn3, cos_c, sin_c, pool, st, wn3, g3)


def _ssel_kernel(pt_ref, idx_ref, q_ref, *refs, nblk):
    blk_refs = refs[:2 * SEL_TOP]
    kn_ref, vn_ref, g_ref, ocw_ref, o_ref = refs[2 * SEL_TOP:]
    b, k = pl.program_id(0), pl.program_id(1)
    qs = _group_rows(q_ref[...])
    sel_head = A_HEADS + B_KV + k
    flat = SEL_LEN * CACHE_HEADS
    own = (lax.broadcasted_iota(I32, (8, flat), 1) % CACHE_HEADS) == sel_head

    s_new = jnp.sum(qs.astype(F32) * kn_ref[...], axis=-1, keepdims=True)
    s_parts = []
    for r in range(SEL_TOP):
        valid = idx_ref[(b * B_KV + k) * SEL_TOP + r] < nblk
        kb = blk_refs[r][...].reshape(flat, HEAD_DIM).astype(BF16)
        s_parts.append(jnp.where(own, jnp.where(valid, _dot_nt(qs, kb), NEG), NEG))
    s = jnp.concatenate(s_parts, axis=1)
    mx = jnp.maximum(jnp.max(s, axis=-1, keepdims=True), s_new)
    p = jnp.exp(s - mx)
    pn = jnp.exp(s_new - mx)
    den = jnp.sum(p, axis=-1, keepdims=True) + pn
    pb = p.astype(BF16)
    acc = pn * vn_ref[...]
    for r in range(SEL_TOP):
        vb = blk_refs[SEL_TOP + r][...].reshape(flat, HEAD_DIM).astype(BF16)
        acc = acc + _dot(pb[:, r * flat:(r + 1) * flat], vb)
    osel = acc / den
    gt = g_ref[...]
    ocw = ocw_ref[...]
    outs = []
    for g in range(B_GROUP):
        outs.append(ocw[:, g * LANES:(g + 1) * LANES] + gt[:, 3 * g + 1:3 * g + 2] * osel[g:g + 1])
    o_ref[...] = jnp.concatenate(outs, axis=1).astype(o_ref.dtype)


def _sample_sel(layer, pt, idx, q3, cache_k, cache_v, kn3, vn3, g3, ocw):
    nb = q3.shape[0]
    per = PAGE_SIZE // SEL_LEN
    nblk = pt.shape[1] * per

    def blk_spec(r):
        def blk_map(b, k, t, ix):
            n = jnp.minimum(ix[(b * B_KV + k) * SEL_TOP + r], nblk - 1)
            return (layer, t[b, n // per], n % per, 0, 0)
        return pl.BlockSpec((None, None, SEL_LEN, CACHE_HEADS, HEAD_DIM), blk_map)

    grid_spec = pltpu.PrefetchScalarGridSpec(
        num_scalar_prefetch=2, grid=(nb, B_KV),
        in_specs=[pl.BlockSpec((None, 1, 4 * HEAD_DIM), lambda b, k, t, ix: (b, 0, 1 + k))]
                 + [blk_spec(r) for r in range(SEL_TOP)] * 2 + [
                  pl.BlockSpec((None, 1, HEAD_DIM), lambda b, k, t, ix: (b, 0, 6 + k)),
                  pl.BlockSpec((None, 1, HEAD_DIM), lambda b, k, t, ix: (b, 0, 6 + k)),
                  pl.BlockSpec((None, 1, LANES), lambda b, k, t, ix: (b, 0, k)),
                  pl.BlockSpec((None, 1, 4 * HEAD_DIM), lambda b, k, t, ix: (b, 0, k))],
        out_specs=pl.BlockSpec((None, 1, 4 * HEAD_DIM), lambda b, k, t, ix: (b, 0, k)))
    return pl.pallas_call(
        functools.partial(_ssel_kernel, nblk=nblk), grid_spec=grid_spec,
        out_shape=jax.ShapeDtypeStruct((nb, 1, B_WIDTH), BF16),
        compiler_params=_cp("parallel", "parallel"), name="sample_sel",
    )(pt, idx, q3, *([cache_k] * SEL_TOP), *([cache_v] * SEL_TOP), kn3, vn3, g3, ocw)


def _spool_kernel(st_ref, u_ref, pw_ref, ps_ref, o_ref, ns_ref):
    u = u_ref[...]
    for g, w in enumerate(C_WINDOWS):
        cols = slice(g * C_GW, (g + 1) * C_GW)
        acc = u[:, cols]
        for r in range(1, w):
            o0 = (POOL_HIST - r) * C_WIDTH + g * C_GW
            acc = acc + st_ref[:, o0:o0 + C_GW]
        d = acc / float(w) - u[:, cols]
        o_ref[:, cols] = (_dot(d.astype(BF16), pw_ref[g]) * ps_ref[:, cols]).astype(o_ref.dtype)
    ns_ref[:, 0:(POOL_HIST - 1) * C_WIDTH] = st_ref[:, C_WIDTH:]
    ns_ref[:, (POOL_HIST - 1) * C_WIDTH:] = u


def _sample_pool(st2, u, pool_w_bf, pool_scale):
    nb = u.shape[0]
    return pl.pallas_call(
        _spool_kernel,
        out_shape=[jax.ShapeDtypeStruct((nb, C_WIDTH), BF16), jax.ShapeDtypeStruct(st2.shape, F32)],
        compiler_params=pltpu.CompilerParams(vmem_limit_bytes=VMEM_LIMIT), name="sample_pool",
    )(st2, u, pool_w_bf, pool_scale.reshape(1, C_WIDTH))


def _sample_mixer(x, x_bf, w, w_out_bf, lam_vec, lam_init, norm_g, pool_w_bf, pool_scale, ln_g, ln_b, alpha,
                  layer, cache_k, cache_v, st_win, st_pool, pt, sconsts):
    nb = x.shape[0]
    past = pt.shape[1] * PAGE_SIZE
    kf, _, vf, _, wf, _, qb, uf, gf = _project_all(x_bf, w, sconsts["tabs"], tm=nb)
    q3, kn3, vn3 = qb.reshape(nb, 1, -1), kf.reshape(nb, 1, -1), vf.reshape(nb, 1, -1)
    oa, kch, vch = _sample_diff(layer, pt, q3, cache_k, cache_v, kn3, vn3, lam_vec, norm_g, lam_init)
    wb = st_win.shape[1]
    idx4, ocw, new_win = _sample_nsa_a(q3, kch, vch, kn3, vn3, sconsts["cos_c"], sconsts["sin_c"], sconsts["pool"],
                                       st_win.reshape(nb, wb, 4 * HEAD_DIM), wf.reshape(nb, 1, -1), gf.reshape(nb, 1, -1), past)
    ob = _sample_sel(layer, pt, idx4[:, :, :, 0].reshape(-1), q3, cache_k, cache_v, kn3, vn3, gf.reshape(nb, 1, -1), ocw)
    oc, new_pool = _sample_pool(st_pool.reshape(nb, POOL_HIST * C_WIDTH), uf, pool_w_bf, pool_scale)
    xn, xn_bf = _outproj_ln(oa.reshape(nb, A_WIDTH), ob.reshape(nb, B_WIDTH), oc, x, w_out_bf, ln_g, ln_b, alpha)
    return xn, xn_bf, kf, vf, new_win, new_pool


def _sample_consts(nb, past):
    pos = jnp.full((nb,), past, I32)
    tabs = _rope_tables(pos)
    nc = past // CMP_STRIDE
    cend = jnp.arange(nc, dtype=I32) * CMP_STRIDE + (CMP_LEN - 1)
    cos_c, sin_c, _, _ = _rope_tables(cend)
    ns = past // SEL_LEN + 1
    nsp = -(-ns // LANES) * LANES
    j = jnp.arange(nc)[:, None]
    n = jnp.arange(nsp)[None, :]
    pool = ((j >= 4 * n - 1) & (j <= 4 * n + 3) & (n < ns)).astype(BF16)
    return dict(tabs=tabs, cos_c=cos_c, sin_c=sin_c, pool=pool)


def kernel(x_prompt, x_sample, cache_k, cache_v, state_win, state_pool, page_table, w_in, w_out, diff_lambda, diff_norm_g, pool_w, pool_scale, ln1_g, ln1_b, ln2_g, ln2_b, ffn_w_gate, ffn_w_up, ffn_w_down, moe_router, moe_w_gate, moe_w_up, moe_w_down):
    depth = w_in.shape[0]
    alpha = (2 * depth) ** 0.25
    s = x_prompt.shape[1]
    nb = x_sample.shape[0]
    past = page_table.shape[1] * PAGE_SIZE
    pconsts = _prompt_consts(s)
    sconsts = _sample_consts(nb, past)
    xp, xs = x_prompt[0], x_sample[:, 0]
    xp_bf, xs_bf = xp.astype(BF16), xs.astype(BF16)
    outs = [[] for _ in range(8)]
    for l in range(depth):
        lam_init = 0.8 - 0.6 * math.exp(-0.3 * l)
        w = _split_w_in(w_in[l])
        w_out_bf = w_out[l].astype(BF16)
        pw_bf = pool_w[l].astype(BF16)
        xp, xp_bf, kf, vf, wf, uf = _prompt_mixer(xp, xp_bf, w, w_out_bf, diff_lambda[l], lam_init, diff_norm_g[l],
                                                  pw_bf, pool_scale[l], ln1_g[l], ln1_b[l], alpha, pconsts)
        xs, xs_bf, skf, svf, swin, spool = _sample_mixer(xs, xs_bf, w, w_out_bf, diff_lambda[l], lam_init, diff_norm_g[l],
                                                         pw_bf, pool_scale[l], ln1_g[l], ln1_b[l], alpha,
                                                         l, cache_k, cache_v, state_win[l], state_pool[l], page_table, sconsts)
        for o, v in zip(outs, (kf, vf, wf, uf, skf, svf, swin, spool)):
            o.append(v)
        i = l // 2
        if l % 2 == 0:
            wg, wu, wd = (a[i:i + 1].astype(BF16) for a in (ffn_w_gate, ffn_w_up, ffn_w_down))
            xp, xp_bf = _ffn_ln(xp_bf, xp, jnp.ones((1, s, 1), F32), wg, wu, wd, ln2_g[l], ln2_b[l], alpha, gated=False)
            xs, xs_bf = _ffn_ln(xs_bf, xs, jnp.ones((1, nb, 1), F32), wg, wu, wd, ln2_g[l], ln2_b[l], alpha, gated=False)
        else:
            wg, wu, wd = moe_w_gate[i], moe_w_up[i], moe_w_down[i]
            xp, xp_bf = _moe_prompt(xp, moe_router[i], wg, wu, wd, ln2_g[l], ln2_b[l], alpha)
            gate_s, _, _ = _router(xs, moe_router[i])
            gate3 = gate_s[:, :N_EXPERTS].T.reshape(N_EXPERTS, nb, 1)
            xs, xs_bf = _ffn_ln(xs_bf, xs, gate3, wg, wu, wd, ln2_g[l], ln2_b[l], alpha, gated=True)
    pk, pv, pwn, ppl, sk, sv, sw, sp = outs
    wbuf = min(WINDOW, s)
    swb = state_win.shape[2]
    return (xp[None], xs[:, None],
            jnp.stack(pk).reshape(depth, 1, s, CACHE_HEADS, HEAD_DIM),
            jnp.stack(pv).reshape(depth, 1, s, CACHE_HEADS, HEAD_DIM),
            jnp.stack(pwn).reshape(depth, 1, wbuf, 2, B_KV, HEAD_DIM),
            jnp.stack(ppl).reshape(depth, 1, POOL_HIST, C_WIDTH),
            jnp.stack(sk).reshape(depth, nb, 1, CACHE_HEADS, HEAD_DIM),
            jnp.stack(sv).reshape(depth, nb, 1, CACHE_HEADS, HEAD_DIM),
            jnp.stack(sw).reshape(depth, nb, swb, 2, B_KV, HEAD_DIM),
            jnp.stack(sp).reshape(depth, nb, POOL_HIST, C_WIDTH))
```
